```python
import functools
import jax
import jax.numpy as jnp
from jax import lax
import numpy as np

D_MODEL = 1024
BATCH = 8
SEQ = 2048
DEPTH = 4
DEC_BATCH = 128
DEC_SEQ = 4
PAST_LEN = 8192
PAGE_SIZE = 128

N_META = 16
N_AB = (DEPTH + 1) // 2
N_POOL = DEPTH // 2
H_A = 8
Q_LORA = 384
KV_LORA = 256
NOPE_DIM = 64
ROPE_DIM = 32
V_DIM = 64
ROPE_BASE = 10000.0
Q_BLOCK = 128
SM_SCALE = (NOPE_DIM + ROPE_DIM) ** -0.5
H_B = 8
N_B = 64
D_B = H_B * N_B
DECAY_LORA = 64
AAA_LORA = 64
GATE_LORA = 160
RWKV_IN = 3 * D_B + DECAY_LORA + AAA_LORA + GATE_LORA
MLA_IN = Q_LORA + KV_LORA + ROPE_DIM
N_IN = MLA_IN + RWKV_IN
D_MIX = H_A * V_DIM + D_B
GN_EPS = 64e-5
POOL_WINDOWS = (2, 4, 8, 16)
N_POOL_GROUPS = 4
POOL_GC = D_MODEL // N_POOL_GROUPS
W_MAX = 16
D_FF = 4 * D_MODEL
RMS_EPS = 1e-6

kernel_name = 'hybrid_mla_rwkv7_pool_step'


def rms_norm(x, g):
    xf = x.astype(jnp.float32)
    y = xf * lax.rsqrt(jnp.mean(xf * xf, axis=-1, keepdims=True) + RMS_EPS)
    return (y * g.astype(jnp.float32)).astype(x.dtype)


def rope_angles(pos):
    half = ROPE_DIM // 2
    inv = ROPE_BASE ** (-jnp.arange(half, dtype=jnp.float32) / half)
    ang = pos[:, None] * inv[None, :]
    return jnp.cos(ang), jnp.sin(ang)


def apply_rope(x, cos, sin):
    half = ROPE_DIM // 2
    xf = x.astype(jnp.float32)
    x1, x2 = xf[..., :half], xf[..., half:]
    return jnp.concatenate([x1 * cos - x2 * sin, x2 * cos + x1 * sin], axis=-1).astype(x.dtype)


def ab_project(h, cos, sin, w_in, g_q, w_uq, g_kv, w_uk):
    b, t, _ = h.shape
    z = h @ w_in
    zq = z[..., :Q_LORA]
    zkv = z[..., Q_LORA:Q_LORA + KV_LORA]
    zpe = z[..., Q_LORA + KV_LORA:MLA_IN]
    zb = z[..., MLA_IN:]
    q = (rms_norm(zq, g_q) @ w_uq).reshape(b, t, H_A, NOPE_DIM + ROPE_DIM)
    q_nope = q[..., :NOPE_DIM]
    q_pe = apply_rope(q[..., NOPE_DIM:], cos[:, None, :], sin[:, None, :])
    q_lat = jnp.einsum('bthn,rhn->bthr', q_nope, w_uk)
    c_kv = rms_norm(zkv, g_kv)
    k_pe = apply_rope(zpe, cos, sin)
    return q_lat, q_pe, c_kv, k_pe, zb


def mla_attend_prompt(q_lat, q_pe, c_kv, k_pe):
    b, seq = c_kv.shape[0], c_kv.shape[1]
    n_blk = -(-seq // Q_BLOCK)
    lp = n_blk * Q_BLOCK

    def to_blocks(u):
        u = jnp.pad(u, ((0, 0), (0, lp - seq), (0, 0), (0, 0)))
        return jnp.swapaxes(u.reshape((b, n_blk, Q_BLOCK) + u.shape[2:]), 0, 1)

    key_pos = jnp.arange(seq)

    def block(args):
        ql, qp, i = args
        q_pos = i * Q_BLOCK + jnp.arange(Q_BLOCK)
        s = jnp.einsum('bqhr,bkr->bhqk', ql, c_kv) + jnp.einsum('bqhp,bkp->bhqk', qp, k_pe)
        s = jnp.where(key_pos[None, :] <= q_pos[:, None], s.astype(jnp.float32) * SM_SCALE, -jnp.inf)
        p = jax.nn.softmax(s, axis=-1).astype(c_kv.dtype)
        return jnp.einsum('bhqk,bkr->bqhr', p, c_kv)

    out = lax.map(block, (to_blocks(q_lat), to_blocks(q_pe), jnp.arange(n_blk)))
    return jnp.swapaxes(out, 0, 1).reshape(b, lp, H_A, KV_LORA)[:, :seq]


def mla_attend_sample(q_lat, q_pe, c_new, kpe_new, *, ckv_pool, kpe_pool, page_table):
    db, n_pages = page_table.shape
    c_past = ckv_pool[page_table].reshape(db, n_pages * PAGE_SIZE, KV_LORA)
    k_past = kpe_pool[page_table].reshape(db, n_pages * PAGE_SIZE, ROPE_DIM)
    t = q_lat.shape[1]
    s_past = jnp.einsum('bqhr,bkr->bhqk', q_lat, c_past) + jnp.einsum('bqhp,bkp->bhqk', q_pe, k_past)
    s_new = jnp.einsum('bqhr,bkr->bhqk', q_lat, c_new) + jnp.einsum('bqhp,bkp->bhqk', q_pe, kpe_new)
    causal = jnp.arange(t)[None, :] <= jnp.arange(t)[:, None]
    s_new = jnp.where(causal, s_new.astype(jnp.float32) * SM_SCALE, -jnp.inf)
    s = jnp.concatenate([s_past.astype(jnp.float32) * SM_SCALE, s_new], axis=-1)
    p = jax.nn.softmax(s, axis=-1).astype(c_new.dtype)
    n_past = c_past.shape[1]
    return (jnp.einsum('bhqk,bkr->bqhr', p[..., :n_past], c_past)
            + jnp.einsum('bhqk,bkr->bqhr', p[..., n_past:], c_new))


def rwkv7_mix(zb, prev_row, s0, mu, w0, w2, a0, a2, g2, k_k, k_a, r_k, ln_w, ln_b):
    b, t, _ = zb.shape
    prev = jnp.concatenate([prev_row[:, None, :].astype(zb.dtype), zb[:, :-1]], axis=1)
    xm = zb + (prev - zb) * mu
    o1, o2, o3 = D_B, 2 * D_B, 3 * D_B
    o4 = o3 + DECAY_LORA
    o5 = o4 + AAA_LORA
    r, k, v = xm[..., :o1], xm[..., o1:o2], xm[..., o2:o3]
    xw, xa, xg = xm[..., o3:o4], xm[..., o4:o5], xm[..., o5:]
    w_log = -jax.nn.softplus(-(w0 + jnp.tanh(xw) @ w2).astype(jnp.float32)) - 0.5
    decay = jnp.exp(-jnp.exp(w_log))
    a = jax.nn.sigmoid((a0 + xa @ a2).astype(jnp.float32))
    g = jax.nn.sigmoid(xg) @ g2

    def heads(u):
        return u.reshape(b, t, H_B, N_B)

    kf = k.astype(jnp.float32)
    kk = heads(kf * k_k)
    kk = kk / jnp.maximum(jnp.sqrt(jnp.sum(kk * kk, axis=-1, keepdims=True)), 1e-12)
    k_eff = heads(kf * (1.0 + (a - 1.0) * k_a))
    a_h = heads(a)
    rf = heads(r.astype(jnp.float32))
    vf = heads(v.astype(jnp.float32))
    dec = heads(decay)

    def step(S, inp):
        r_t, w_t, k_t, v_t, kk_t, a_t = inp
        sa = jnp.einsum('bhvk,bhk->bhv', S, kk_t)
        S = (S * w_t[:, :, None, :] - sa[..., None] * (kk_t * a_t)[:, :, None, :]
             + v_t[..., None] * k_t[:, :, None, :])
        return S, jnp.einsum('bhvk,bhk->bhv', S, r_t)

    xs_t = (jnp.moveaxis(rf, 1, 0), jnp.moveaxis(dec, 1, 0), jnp.moveaxis(k_eff, 1, 0),
            jnp.moveaxis(vf, 1, 0), jnp.moveaxis(kk, 1, 0), jnp.moveaxis(a_h, 1, 0))
    s_fin, y = lax.scan(step, s0.astype(jnp.float32), xs_t)
    y = jnp.moveaxis(y, 0, 1)
    mean = jnp.mean(y, axis=-1, keepdims=True)
    var = jnp.mean(jnp.square(y - mean), axis=-1, keepdims=True)
    y = ((y - mean) * lax.rsqrt(var + GN_EPS)).reshape(b, t, D_B) * ln_w + ln_b
    bonus = jnp.sum(rf * k_eff * r_k, axis=-1, keepdims=True) * vf
    out = (y + bonus.reshape(b, t, D_B)) * g.astype(jnp.float32)
    return out.astype(zb.dtype), s_fin.astype(s0.dtype), zb[:, -1]


def ab_mixer(h, cos, sin, attend, prev_row, s0, proj, rw):
    w_in, g_q, w_uq, g_kv, w_uk, w_uv, w_out = proj
    b, t, _ = h.shape
    q_lat, q_pe, c_kv, k_pe, zb = ab_project(h, cos, sin, w_in, g_q, w_uq, g_kv, w_uk)
    o_lat = attend(q_lat, q_pe, c_kv, k_pe)
    o_a = jnp.einsum('bthr,rhv->bthv', o_lat, w_uv).reshape(b, t, H_A * V_DIM)
    o_b, s_fin, last = rwkv7_mix(zb, prev_row, s0, *rw)
    y = jnp.concatenate([o_a, o_b], axis=-1) @ w_out
    return y, c_kv, k_pe, s_fin, last


def pool_mixer(h, prefix, pos0, w_pool, pool_scale):
    b, t, d = h.shape
    p = prefix.shape[1]
    full = jnp.concatenate([prefix.astype(h.dtype), h], axis=1)
    cs = jnp.cumsum(jnp.pad(full.astype(jnp.float32), ((0, 0), (W_MAX, 0), (0, 0))), axis=1)
    end = cs[:, p + W_MAX:p + W_MAX + t]
    hf = h.astype(jnp.float32)
    pos = pos0 + jnp.arange(t)
    outs = []
    for gi, w in enumerate(POOL_WINDOWS):
        sl = slice(gi * POOL_GC, (gi + 1) * POOL_GC)
        cnt = jnp.minimum(pos + 1, w).astype(jnp.float32)[:, None]
        win = end[..., sl] - cs[:, p + W_MAX - w:p + W_MAX - w + t, sl]
        outs.append(win / cnt - hf[..., sl])
    pooled = jnp.concatenate(outs, axis=-1).astype(h.dtype).reshape(b, t, N_POOL_GROUPS, POOL_GC)
    y = jnp.einsum('btgc,gcd->btgd', pooled, w_pool).reshape(b, t, d) * pool_scale
    return y, full[:, -(W_MAX - 1):]


def channel_mixer(x, g_pre, g_post, w_up, w_down):
    h = rms_norm(x, g_pre)
    y = jnp.square(jax.nn.relu(h @ w_up)) @ w_down
    return x + rms_norm(y, g_post)


def setup_inputs(seed: int = 0) -> dict:
    key = jax.random.key(seed)
    ks = jax.random.split(key, 40)
    f32 = jnp.float32

    def nrm(k, shape, scale=1.0):
        return jax.random.normal(k, shape, f32) * scale

    def gain(k, shape):
        return 1.0 + nrm(k, shape, 0.05)

    n_pages = PAST_LEN // PAGE_SIZE
    n_used = DEC_BATCH * n_pages
    n_pool_pages = n_used + n_used // 4
    page_table = jax.random.permutation(ks[0], n_pool_pages)[:n_used].reshape(DEC_BATCH, n_pages).astype(jnp.int32)
    return {
        'x_prompt': nrm(ks[1], (BATCH, SEQ, D_MODEL)),
        'x_sample': nrm(ks[2], (DEC_BATCH, DEC_SEQ, D_MODEL)),
        'cache_ckv': nrm(ks[3], (N_AB, n_pool_pages, PAGE_SIZE, KV_LORA)),
        'cache_kpe': nrm(ks[4], (N_AB, n_pool_pages, PAGE_SIZE, ROPE_DIM)),
        'page_table': page_table,
        'state_wkv': nrm(ks[5], (N_AB, DEC_BATCH, H_B, N_B, N_B), 0.3),
        'state_shift': nrm(ks[6], (N_AB, DEC_BATCH, RWKV_IN)),
        'state_pool': nrm(ks[7], (N_POOL, DEC_BATCH, W_MAX - 1, D_MODEL)),
        'meta_tokens': nrm(ks[8], (N_META, D_MODEL)),
        'g_mix_pre': gain(ks[9], (DEPTH, D_MODEL)),
        'g_mix_post': gain(ks[10], (DEPTH, D_MODEL)),
        'g_ffn_pre': gain(ks[11], (DEPTH, D_MODEL)),
        'g_ffn_post': gain(ks[12], (DEPTH, D_MODEL)),
        'w_in': nrm(ks[13], (N_AB, D_MODEL, N_IN), D_MODEL ** -0.5),
        'g_q': gain(ks[14], (N_AB, Q_LORA)),
        'w_uq': nrm(ks[15], (N_AB, Q_LORA, H_A * (NOPE_DIM + ROPE_DIM)), Q_LORA ** -0.5),
        'g_kv': gain(ks[16], (N_AB, KV_LORA)),
        'w_uk': nrm(ks[17], (N_AB, KV_LORA, H_A, NOPE_DIM), KV_LORA ** -0.5),
        'w_uv': nrm(ks[18], (N_AB, KV_LORA, H_A, V_DIM), KV_LORA ** -0.5),
        'mu_shift': jax.random.uniform(ks[19], (N_AB, RWKV_IN), f32),
        'w0': jax.random.uniform(ks[20], (N_AB, D_B), f32, minval=-6.0, maxval=1.0),
        'w2': nrm(ks[21], (N_AB, DECAY_LORA, D_B), 0.1),
        'a0': nrm(ks[22], (N_AB, D_B), 0.5),
        'a2': nrm(ks[23], (N_AB, AAA_LORA, D_B), 0.5 * AAA_LORA ** -0.5),
        'g2': nrm(ks[24], (N_AB, GATE_LORA, D_B), GATE_LORA ** -0.5),
        'k_k': 0.85 + nrm(ks[25], (N_AB, D_B), 0.05),
        'k_a': 1.0 + nrm(ks[26], (N_AB, D_B), 0.05),
        'r_k': nrm(ks[27], (N_AB, H_B, N_B), 0.1),
        'ln_w': gain(ks[28], (N_AB, D_B)),
        'ln_b': nrm(ks[29], (N_AB, D_B), 0.01),
        'w_out': nrm(ks[30], (N_AB, D_MIX, D_MODEL), D_MIX ** -0.5),
        'w_pool': nrm(ks[31], (N_POOL, N_POOL_GROUPS, POOL_GC, POOL_GC), POOL_GC ** -0.5),
        'pool_scale': 1.0 + nrm(ks[32], (N_POOL, D_MODEL), 0.1),
        'w_up': nrm(ks[33], (DEPTH, D_MODEL, D_FF), D_MODEL ** -0.5),
        'w_down': nrm(ks[34], (DEPTH, D_FF, D_MODEL), D_FF ** -0.5),
    }


def reference(x_prompt, x_sample, cache_ckv, cache_kpe, page_table, state_wkv, state_shift, state_pool,
              meta_tokens, g_mix_pre, g_mix_post, g_ffn_pre, g_ffn_post,
              w_in, g_q, w_uq, g_kv, w_uk, w_uv, mu_shift, w0, w2, a0, a2, g2, k_k, k_a, r_k,
              ln_w, ln_b, w_out, w_pool, pool_scale, w_up, w_down):
    f32 = jnp.float32
    bp = x_prompt.shape[0]
    meta = jnp.broadcast_to(meta_tokens[None].astype(x_prompt.dtype), (bp, N_META, D_MODEL))
    xp = jnp.concatenate([meta, x_prompt], axis=1)
    xs = x_sample
    seq_p = xp.shape[1]
    t_s = xs.shape[1]
    past = page_table.shape[1] * PAGE_SIZE
    cos_p, sin_p = rope_angles(jnp.arange(seq_p, dtype=f32))
    cos_s, sin_s = rope_angles(past + jnp.arange(t_s, dtype=f32))

    ckv_p, kpe_p, wkv_p, shift_p, pool_p = [], [], [], [], []
    ckv_s, kpe_s, wkv_s, shift_s, pool_s = [], [], [], [], []
    for l in range(DEPTH):
        if l % 2 == 0:
            e = l // 2
            proj = (w_in[e], g_q[e], w_uq[e], g_kv[e], w_uk[e], w_uv[e], w_out[e])
            rw = (mu_shift[e], w0[e], w2[e], a0[e], a2[e], g2[e], k_k[e], k_a[e], r_k[e], ln_w[e], ln_b[e])
            y, c, kp, s_fin, last = ab_mixer(
                rms_norm(xp, g_mix_pre[l]), cos_p, sin_p, mla_attend_prompt,
                jnp.zeros((bp, RWKV_IN), xp.dtype), jnp.zeros((bp, H_B, N_B, N_B), xp.dtype), proj, rw)
            xp = xp + rms_norm(y, g_mix_post[l])
            ckv_p.append(c)
            kpe_p.append(kp)
            wkv_p.append(s_fin)
            shift_p.append(last)
            attend_s = functools.partial(mla_attend_sample, ckv_pool=cache_ckv[e], kpe_pool=cache_kpe[e],
                                         page_table=page_table)
            y, c, kp, s_fin, last = ab_mixer(
                rms_norm(xs, g_mix_pre[l]), cos_s, sin_s, attend_s, state_shift[e], state_wkv[e], proj, rw)
            xs = xs + rms_norm(y, g_mix_post[l])
            ckv_s.append(c)
            kpe_s.append(kp)
            wkv_s.append(s_fin)
            shift_s.append(last)
        else:
            o = l // 2
            h = rms_norm(xp, g_mix_pre[l])
            y, buf = pool_mixer(h, h[:, :0], 0, w_pool[o], pool_scale[o])
            xp = xp + rms_norm(y, g_mix_post[l])
            pool_p.append(buf)
            y, buf = pool_mixer(rms_norm(xs, g_mix_pre[l]), state_pool[o], past, w_pool[o], pool_scale[o])
            xs = xs + rms_norm(y, g_mix_post[l])
            pool_s.append(buf)
        xp = channel_mixer(xp, g_ffn_pre[l], g_ffn_post[l], w_up[l], w_down[l])
        xs = channel_mixer(xs, g_ffn_pre[l], g_ffn_post[l], w_up[l], w_down[l])

    y_prompt = xp[:, N_META:]
    y_sample = xs
    new_ckv_p = jnp.stack(ckv_p)
    new_kpe_p = jnp.stack(kpe_p)
    new_wkv_p = jnp.stack(wkv_p)
    new_shift_p = jnp.stack(shift_p)
    new_pool_p = jnp.stack(pool_p)
    new_ckv_s = jnp.stack(ckv_s)
    new_kpe_s = jnp.stack(kpe_s)
    new_wkv_s = jnp.stack(wkv_s)
    new_shift_s = jnp.stack(shift_s)
    new_pool_s = jnp.stack(pool_s)
    return (y_prompt, y_sample, new_ckv_p, new_kpe_p, new_wkv_p, new_shift_p, new_pool_p,
            new_ckv_s, new_kpe_s, new_wkv_s, new_shift_s, new_pool_s)
```

```python
import functools

import jax
import jax.numpy as jnp
from jax import lax
from jax.experimental import pallas as pl
from jax.experimental.pallas import tpu as pltpu

F32 = jnp.float32
BF16 = jnp.bfloat16

D_MODEL = 1024
N_META = 16
PAGE_SIZE = 128
H_A = 8
Q_LORA = 384
KV_LORA = 256
NOPE_DIM = 64
ROPE_DIM = 32
V_DIM = 64
ROPE_BASE = 10000.0
SM_SCALE = (NOPE_DIM + ROPE_DIM) ** -0.5
H_B = 8
N_B = 64
D_B = H_B * N_B
DECAY_LORA = 64
AAA_LORA = 64
GATE_LORA = 160
RWKV_IN = 3 * D_B + DECAY_LORA + AAA_LORA + GATE_LORA
MLA_IN = Q_LORA + KV_LORA + ROPE_DIM
GN_EPS = 64e-5
POOL_WINDOWS = (2, 4, 8, 16)
POOL_GC = D_MODEL // len(POOL_WINDOWS)
W_MAX = 16
D_FF = 4 * D_MODEL
RMS_EPS = 1e-6

LANES = 128
QK_PAD = KV_LORA + LANES
VMEM_LIMIT = 56 * 1024 * 1024
ROW_TILE = 384
ATT_Q_TILE = 688
POOL_TILE = 688
SCAN_CHUNK_P = 48
SCAN_CHUNK_S = 16
PAGES_PER_CHUNK = 8


def _cparams(sem):
    return pltpu.CompilerParams(dimension_semantics=sem, vmem_limit_bytes=VMEM_LIMIT)


def _rms(x, g):
    return x * lax.rsqrt(jnp.mean(x * x, axis=-1, keepdims=True) + RMS_EPS) * g


def _dot(a, b):
    return jnp.dot(a, b, preferred_element_type=F32)


def _dot_nt(a, b):
    return lax.dot_general(a, b, (((1,), (1,)), ((), ())), preferred_element_type=F32)


def _dot_tn(a, b):
    return lax.dot_general(a, b, (((0,), (0,)), ((), ())), preferred_element_type=F32)


def _full(shape):
    n = len(shape)
    return pl.BlockSpec(shape, lambda *_: (0,) * n)


def _ab_in_kernel(x_ref, gpre_ref, wq2_ref, wkv_ref, wb_ref, gq_ref, gkv_ref, wuq_ref, wuk_ref, cos_ref, sin_ref,
                  ckv_ref, kpe_ref, zb_ref, qabs_ref):
    h = _rms(x_ref[...], gpre_ref[...]).astype(BF16)
    zq2 = _dot(h, wq2_ref[...])
    zkv = _dot(h, wkv_ref[...])
    zb_ref[...] = _dot(h, wb_ref[...])
    ckv_ref[...] = _rms(zkv, gkv_ref[...])
    cos = cos_ref[...]
    sin = sin_ref[...]
    pe = zq2[:, Q_LORA:Q_LORA + ROPE_DIM]
    pe_sw = zq2[:, Q_LORA + ROPE_DIM:Q_LORA + 2 * ROPE_DIM]
    kpe_ref[...] = pe * cos[:, :ROPE_DIM] + pe_sw * sin[:, :ROPE_DIM]
    qn = _rms(zq2[:, :Q_LORA], gq_ref[...]).astype(BF16)
    q2 = _dot(qn, wuq_ref[...])
    n_nope = H_A * NOPE_DIM
    for hd in range(H_A):
        qn_h = q2[:, hd * NOPE_DIM:(hd + 1) * NOPE_DIM].astype(BF16)
        qabs_ref[hd, :, 0:KV_LORA] = _dot(qn_h, wuk_ref[hd]).astype(BF16)
        r0 = n_nope + hd * LANES
        r1 = n_nope + (H_A + hd) * LANES
        qabs_ref[hd, :, KV_LORA:QK_PAD] = (q2[:, r0:r0 + LANES] * cos + q2[:, r1:r1 + LANES] * sin).astype(BF16)


def _ab_in_proj(x, g_pre, wts, cos_t, sin_t, tm):
    m = x.shape[0]
    wq2, wkv, wb, g_q, g_kv, wuq, wuk = wts
    row = lambda w: pl.BlockSpec((tm, w), lambda i: (i, 0))
    return pl.pallas_call(
        _ab_in_kernel,
        grid=(m // tm,),
        in_specs=[row(D_MODEL), _full(g_pre.shape), _full(wq2.shape), _full(wkv.shape), _full(wb.shape),
                  _full(g_q.shape), _full(g_kv.shape), _full(wuq.shape), _full(wuk.shape), row(LANES), row(LANES)],
        out_specs=[row(KV_LORA), row(ROPE_DIM), row(RWKV_IN),
                   pl.BlockSpec((H_A, tm, QK_PAD), lambda i: (0, i, 0))],
        out_shape=[jax.ShapeDtypeStruct((m, KV_LORA), F32), jax.ShapeDtypeStruct((m, ROPE_DIM), F32),
                   jax.ShapeDtypeStruct((m, RWKV_IN), F32), jax.ShapeDtypeStruct((H_A, m, QK_PAD), BF16)],
        compiler_params=_cparams(("parallel",)),
        name="ab_in_proj",
    )(x, g_pre, wq2, wkv, wb, g_q, g_kv, wuq, wuk, cos_t, sin_t)


def _attn_prompt_kernel(q_ref, kv_ref, o_ref, *, t_p, tq):
    for i in range(t_p // tq):
        kext = min(-(-((i + 1) * tq) // LANES) * LANES, kv_ref.shape[1])
        q = q_ref[0, i * tq:(i + 1) * tq, :]
        k = kv_ref[0, :kext, :]
        s = _dot_nt(q, k)
        q_pos = i * tq + lax.broadcasted_iota(jnp.int32, (tq, kext), 0)
        k_pos = lax.broadcasted_iota(jnp.int32, (tq, kext), 1)
        s = jnp.where(k_pos <= q_pos, s * SM_SCALE, -jnp.inf)
        p = jnp.exp(s - jnp.max(s, axis=-1, keepdims=True))
        l = jnp.sum(p, axis=-1, keepdims=True)
        o = _dot(p.astype(BF16), k[:, :KV_LORA])
        o_ref[0, i * tq:(i + 1) * tq, :] = (o / l).astype(BF16)


def _attn_prompt(qabs, kv, t_p):
    n_b, t_pad = kv.shape[0], kv.shape[1]
    m = qabs.shape[1]
    return pl.pallas_call(
        functools.partial(_attn_prompt_kernel, t_p=t_p, tq=ATT_Q_TILE),
        grid=(n_b, H_A),
        in_specs=[pl.BlockSpec((1, t_p, QK_PAD), lambda b, h: (h, b, 0)),
                  pl.BlockSpec((1, t_pad, QK_PAD), lambda b, h: (b, 0, 0))],
        out_specs=pl.BlockSpec((1, t_p, KV_LORA), lambda b, h: (h, b, 0)),
        out_shape=jax.ShapeDtypeStruct((H_A, m, KV_LORA), BF16),
        compiler_params=_cparams(("parallel", "parallel")),
        name="attn_prompt",
    )(qabs, kv)


def _attn_sample_kernel(pt_ref, q_ref, knew_ref, ckv_hbm, kpe_hbm, o_ref, ckv_buf, kpe_buf, sems, *, layer, n_pages,
                        t_s):
    b = pl.program_id(0)
    n_b = pl.num_programs(0)
    slot = b % 2

    def page_copies(bb, sl, p):
        page = pt_ref[bb, p]
        return (pltpu.make_async_copy(ckv_hbm.at[layer, page], ckv_buf.at[sl, p], sems.at[0, sl]),
                pltpu.make_async_copy(kpe_hbm.at[layer, page], kpe_buf.at[sl, p], sems.at[1, sl]))

    def start_all(bb, sl):
        for p in range(n_pages):
            for cp in page_copies(bb, sl, p):
                cp.start()

    @pl.when(b == 0)
    def _():
        start_all(0, 0)

    @pl.when(b + 1 < n_b)
    def _():
        start_all(b + 1, 1 - slot)

    for p in range(n_pages):
        for cp in page_copies(b, slot, p):
            cp.wait()

    q = q_ref[0]
    q_lat = q[:, :KV_LORA]
    q_pe = q[:, KV_LORA:KV_LORA + ROPE_DIM]
    n_q = q.shape[0]
    rows = PAGES_PER_CHUNK * PAGE_SIZE

    def chunk(c, carry):
        m_i, l_i, acc = carry
        p0 = pl.multiple_of(c * PAGES_PER_CHUNK, PAGES_PER_CHUNK)
        kc = ckv_buf[slot, pl.ds(p0, PAGES_PER_CHUNK)].reshape(rows, KV_LORA).astype(BF16)
        kp = kpe_buf[slot, pl.ds(p0, PAGES_PER_CHUNK)].reshape(rows, ROPE_DIM).astype(BF16)
        s = (_dot_nt(q_lat, kc) + _dot_nt(q_pe, kp)) * SM_SCALE
        m_n = jnp.maximum(m_i, jnp.max(s, axis=-1, keepdims=True))
        alpha = jnp.exp(m_i - m_n)
        p = jnp.exp(s - m_n)
        l_n = alpha * l_i + jnp.sum(p, axis=-1, keepdims=True)
        acc_n = alpha * acc + _dot(p.astype(BF16), kc)
        return m_n, l_n, acc_n

    init = (jnp.full((n_q, 1), -jnp.inf, F32), jnp.zeros((n_q, 1), F32), jnp.zeros((n_q, KV_LORA), F32))
    m_i, l_i, acc = lax.fori_loop(0, n_pages // PAGES_PER_CHUNK, chunk, init)

    kn = knew_ref[0]
    s = _dot_nt(q, kn) * SM_SCALE
    step = lax.broadcasted_iota(jnp.int32, s.shape, 0) % t_s
    key = lax.broadcasted_iota(jnp.int32, s.shape, 1)
    s = jnp.where(key <= step, s, -jnp.inf)
    m_n = jnp.maximum(m_i, jnp.max(s, axis=-1, keepdims=True))
    alpha = jnp.exp(m_i - m_n)
    p = jnp.exp(s - m_n)
    l_n = alpha * l_i + jnp.sum(p, axis=-1, keepdims=True)
    acc = alpha * acc + _dot(p.astype(BF16), kn[:, :KV_LORA])
    o_ref[0] = (acc / l_n).astype(BF16)


def _attn_sample(page_table, q_s, k_new, cache_ckv, cache_kpe, layer, t_s):
    n_b, n_pages = page_table.shape
    n_q = q_s.shape[1]
    grid_spec = pltpu.PrefetchScalarGridSpec(
        num_scalar_prefetch=1,
        grid=(n_b,),
        in_specs=[pl.BlockSpec((1, n_q, QK_PAD), lambda b, pt: (b, 0, 0)),
                  pl.BlockSpec((1,) + k_new.shape[1:], lambda b, pt: (b, 0, 0)),
                  pl.BlockSpec(memory_space=pl.ANY),
                  pl.BlockSpec(memory_space=pl.ANY)],
        out_specs=pl.BlockSpec((1, n_q, KV_LORA), lambda b, pt: (b, 0, 0)),
        scratch_shapes=[pltpu.VMEM((2, n_pages, PAGE_SIZE, KV_LORA), F32),
                        pltpu.VMEM((2, n_pages, PAGE_SIZE, ROPE_DIM), F32),
                        pltpu.SemaphoreType.DMA((2, 2))],
    )
    return pl.pallas_call(
        functools.partial(_attn_sample_kernel, layer=layer, n_pages=n_pages, t_s=t_s),
        grid_spec=grid_spec,
        out_shape=jax.ShapeDtypeStruct((n_b, n_q, KV_LORA), BF16),
        compiler_params=_cparams(("arbitrary",)),
        name="attn_sample",
    )(page_table, q_s, k_new, cache_ckv, cache_kpe)


def _ab_out_kernel(olat_ref, ob_ref, x_ref, wuv_ref, wout_ref, gpost_ref, o_ref):
    n_a = H_A * V_DIM
    y = _dot(ob_ref[...].astype(BF16), wout_ref[n_a:, :])
    for hd in range(H_A):
        oa = _dot(olat_ref[hd], wuv_ref[hd]).astype(BF16)
        y = y + _dot(oa, wout_ref[hd * V_DIM:(hd + 1) * V_DIM, :])
    o_ref[...] = x_ref[...] + _rms(y, gpost_ref[...])


def _ab_out_proj(olat, o_b, x, wuv, wout, g_post, tm):
    m = x.shape[0]
    row = lambda w: pl.BlockSpec((tm, w), lambda i: (i, 0))
    return pl.pallas_call(
        _ab_out_kernel,
        grid=(m // tm,),
        in_specs=[pl.BlockSpec((H_A, tm, KV_LORA), lambda i: (0, i, 0)), row(D_B), row(D_MODEL),
                  _full(wuv.shape), _full(wout.shape), _full(g_post.shape)],
        out_specs=row(D_MODEL),
        out_shape=jax.ShapeDtypeStruct((m, D_MODEL), F32),
        compiler_params=_cparams(("parallel",)),
        name="ab_out_proj",
    )(olat, o_b, x, wuv, wout, g_post)


def _ffn_kernel(x_ref, gpre_ref, wup_ref, wdown_ref, gpost_ref, o_ref, *, ff_chunk):
    x = x_ref[...]
    h = _rms(x, gpre_ref[...]).astype(BF16)
    y = jnp.zeros(x.shape, F32)
    for c in range(D_FF // ff_chunk):
        u = _dot(h, wup_ref[:, c * ff_chunk:(c + 1) * ff_chunk])
        a = jnp.square(jnp.maximum(u, 0.0)).astype(BF16)
        y = y + _dot(a, wdown_ref[c * ff_chunk:(c + 1) * ff_chunk, :])
    o_ref[...] = x + _rms(y, gpost_ref[...])


def _ffn(x, g_pre, w_up, w_down, g_post, tm):
    m = x.shape[0]
    row = pl.BlockSpec((tm, D_MODEL), lambda i: (i, 0))
    return pl.pallas_call(
        functools.partial(_ffn_kernel, ff_chunk=1024),
        grid=(m // tm,),
        in_specs=[row, _full(g_pre.shape), _full(w_up.shape), _full(w_down.shape), _full(g_post.shape)],
        out_specs=row,
        out_shape=jax.ShapeDtypeStruct((m, D_MODEL), F32),
        compiler_params=_cparams(("parallel",)),
        name="ffn",
    )(x, g_pre, w_up, w_down, g_post)


def _split_dot(x, w):
    hi = x.astype(BF16)
    lo = (x - hi.astype(F32)).astype(BF16)
    return _dot(hi, w) + _dot(lo, w)


def _rwkv_prep_kernel(zb_ref, prev_ref, mu_ref, w0_ref, w2_ref, a0_ref, a2_ref, g2_ref, kk_ref, ka_ref, rk_ref,
                      ones_ref, r_o, lw_o, keff_o, v_o, kn_o, b_o, g_o, bonus_o):
    zb = zb_ref[...]
    xm = zb + (prev_ref[...] - zb) * mu_ref[...]
    o3 = 3 * D_B
    o4 = o3 + DECAY_LORA
    o5 = o4 + AAA_LORA
    r = xm[:, :D_B]
    k = xm[:, D_B:2 * D_B]
    v = xm[:, 2 * D_B:o3]
    z = -(w0_ref[...] + _dot(jnp.tanh(xm[:, o3:o4]).astype(BF16), w2_ref[...]))
    softplus = jnp.maximum(z, 0.0) + jnp.log(1.0 + jnp.exp(-jnp.abs(z)))
    lw_o[...] = -jnp.exp(-softplus - 0.5)
    a = 1.0 / (1.0 + jnp.exp(-(a0_ref[...] + _dot(xm[:, o4:o5].astype(BF16), a2_ref[...]))))
    sg = 1.0 / (1.0 + jnp.exp(-xm[:, o5:]))
    g_o[...] = _dot(sg.astype(BF16), g2_ref[...])
    ones = ones_ref[...]
    kk = k * kk_ref[...]
    kk = kk / jnp.maximum(jnp.sqrt(_split_dot(kk * kk, ones)), 1e-12)
    keff = k * (1.0 + (a - 1.0) * ka_ref[...])
    r_o[...] = r
    keff_o[...] = keff
    v_o[...] = v
    kn_o[...] = kk
    b_o[...] = kk * a
    bonus_o[...] = _split_dot(r * keff * rk_ref[...], ones) * v


def _rwkv_prep(zb, prev, rw, tm):
    m = zb.shape[0]
    row = lambda w: pl.BlockSpec((tm, w), lambda i: (i, 0))
    return pl.pallas_call(
        _rwkv_prep_kernel,
        grid=(m // tm,),
        in_specs=[row(RWKV_IN), row(RWKV_IN)] + [_full(w.shape) for w in rw],
        out_specs=[row(D_B)] * 8,
        out_shape=[jax.ShapeDtypeStruct((m, D_B), F32)] * 8,
        compiler_params=_cparams(("parallel",)),
        name="rwkv_prep",
    )(zb, prev, *rw)


def _rwkv_scan_kernel(r_ref, lw_ref, keff_ref, v_ref, kn_ref, b_ref, g_ref, bonus_ref, lnw_ref, lnb_ref, s0_ref,
                      tri_ref, o_ref, s_out_ref, s_scr, *, chunk):
    c = pl.program_id(1)

    @pl.when(c == 0)
    def _():
        s_scr[...] = s0_ref[0]

    lw = lw_ref[...]
    hi = lw.astype(BF16)
    r1 = lw - hi.astype(F32)
    mid = r1.astype(BF16)
    lo = (r1 - mid.astype(F32)).astype(BF16)
    tri = tri_ref[...]
    cs = _dot(tri, hi) + _dot(tri, mid) + _dot(tri, lo)
    g_inv = jnp.exp(-cs)
    kq_all = kn_ref[...] * jnp.exp(cs - lw)
    rq_all = r_ref[...] * jnp.exp(cs)
    bk_all = b_ref[...] * g_inv
    kk_all = keff_ref[...] * g_inv
    g_last = jnp.exp(cs[chunk - 1:chunk, :])
    v_all = v_ref[...]

    t_i = lax.broadcasted_iota(jnp.int32, (chunk, chunk), 0)
    s_i = lax.broadcasted_iota(jnp.int32, (chunk, chunk), 1)
    strict = s_i < t_i
    incl = s_i <= t_i
    eye = jnp.where(s_i == t_i, 1.0, 0.0).astype(F32)
    levels = []
    k_lvl = 1
    while (1 << (k_lvl - 1)) < chunk:
        same_new = (t_i >> k_lvl) == (s_i >> k_lvl)
        same_old = (t_i >> (k_lvl - 1)) == (s_i >> (k_lvl - 1))
        levels.append(strict & same_new & jnp.logical_not(same_old))
        k_lvl += 1

    outs = []
    for hd in range(H_B):
        sl = slice(hd * N_B, (hd + 1) * N_B)
        kq = kq_all[:, sl].astype(BF16)
        rq = rq_all[:, sl].astype(BF16)
        bk = bk_all[:, sl].astype(BF16)
        kk = kk_all[:, sl].astype(BF16)
        v = v_all[:, sl].astype(BF16)
        s_h = s_scr[hd]
        s_b = s_h.astype(BF16)
        a_m = -_dot_nt(kq, bk)
        b_m = jnp.where(strict, _dot_nt(kq, kk), 0.0)
        ar = jnp.where(incl, -_dot_nt(rq, bk), 0.0)
        br = jnp.where(incl, _dot_nt(rq, kk), 0.0)
        t_m = eye + jnp.where(levels[0], a_m, 0.0)
        for lvl in levels[1:]:
            x_m = _dot(jnp.where(lvl, a_m, 0.0).astype(BF16), t_m.astype(BF16))
            t_m = t_m + _dot(t_m.astype(BF16), x_m.astype(BF16))
        rhs = _dot_nt(kq, s_b) + _dot(b_m.astype(BF16), v)
        u = _dot(t_m.astype(BF16), rhs.astype(BF16))
        u_b = u.astype(BF16)
        y = _dot_nt(rq, s_b) + _dot(ar.astype(BF16), u_b) + _dot(br.astype(BF16), v)
        s_new = (s_h + _dot_tn(v, kk) - _dot_tn(u_b, bk)) * g_last[:, sl]
        s_scr[hd] = s_new
        s_out_ref[0, hd] = s_new
        mean = jnp.mean(y, axis=-1, keepdims=True)
        var = jnp.mean(jnp.square(y - mean), axis=-1, keepdims=True)
        outs.append((y - mean) * lax.rsqrt(var + GN_EPS))
    yn = jnp.concatenate(outs, axis=-1) * lnw_ref[...] + lnb_ref[...]
    o_ref[...] = (yn + bonus_ref[...]) * g_ref[...]


def _rwkv_scan(streams, ln_w, ln_b, s0, chunk):
    m = streams[0].shape[0]
    n_seq = s0.shape[0]
    n_chunks = m // (n_seq * chunk)
    tri = jnp.tril(jnp.ones((chunk, chunk), F32)).astype(BF16)
    row = pl.BlockSpec((chunk, D_B), lambda b, c: (b * n_chunks + c, 0))
    st = pl.BlockSpec((1, H_B, N_B, N_B), lambda b, c: (b, 0, 0, 0))
    return pl.pallas_call(
        functools.partial(_rwkv_scan_kernel, chunk=chunk),
        grid=(n_seq, n_chunks),
        in_specs=[row] * 8 + [_full(ln_w.shape), _full(ln_b.shape), st, _full(tri.shape)],
        out_specs=[row, st],
        out_shape=[jax.ShapeDtypeStruct((m, D_B), F32), jax.ShapeDtypeStruct(s0.shape, F32)],
        scratch_shapes=[pltpu.VMEM((H_B, N_B, N_B), F32)],
        compiler_params=_cparams(("parallel", "arbitrary")),
        name="rwkv_scan",
    )(*streams, ln_w, ln_b, s0, tri)


def _pool_prompt_kernel(x_ref, gpre_ref, wpool_ref, pscale_ref, gpost_ref, o_ref, tail_ref, hext, *, tm):
    j = pl.program_id(1)
    x = x_ref[...]
    h = _rms(x, gpre_ref[...])

    @pl.when(j == 0)
    def _():
        hext[0:W_MAX, :] = jnp.zeros((W_MAX, D_MODEL), F32)

    @pl.when(j > 0)
    def _():
        hext[0:W_MAX, :] = hext[tm:tm + W_MAX, :]

    hext[W_MAX:, :] = h
    tail_ref[0] = h[tm - W_MAX:, :]
    pos = j * tm + lax.broadcasted_iota(jnp.int32, (tm, 1), 0)
    acc = hext[...]
    ys = []
    for gi, w in enumerate(POOL_WINDOWS):
        acc = acc[:, POOL_GC * (1 if gi else 0):]
        acc = acc + pltpu.roll(acc, w // 2, 0)
        cnt = jnp.minimum(pos + 1, w).astype(F32)
        pooled = acc[W_MAX:, :POOL_GC] / cnt - h[:, gi * POOL_GC:(gi + 1) * POOL_GC]
        ys.append(_dot(pooled.astype(BF16), wpool_ref[gi]))
    y = jnp.concatenate(ys, axis=-1) * pscale_ref[...]
    o_ref[...] = x + _rms(y, gpost_ref[...])


def _pool_prompt(x, g_pre, w_pool, p_scale, g_post, n_b, t_p):
    m = x.shape[0]
    tm = POOL_TILE
    n_t = t_p // tm
    row = pl.BlockSpec((tm, D_MODEL), lambda b, j: (b * n_t + j, 0))
    return pl.pallas_call(
        functools.partial(_pool_prompt_kernel, tm=tm),
        grid=(n_b, n_t),
        in_specs=[row, _full(g_pre.shape), _full(w_pool.shape), _full(p_scale.shape), _full(g_post.shape)],
        out_specs=[row, pl.BlockSpec((1, W_MAX, D_MODEL), lambda b, j: (b, 0, 0))],
        out_shape=[jax.ShapeDtypeStruct((m, D_MODEL), F32), jax.ShapeDtypeStruct((n_b, W_MAX, D_MODEL), F32)],
        scratch_shapes=[pltpu.VMEM((W_MAX + tm, D_MODEL), F32)],
        compiler_params=_cparams(("parallel", "arbitrary")),
        name="pool_prompt",
    )(x, g_pre, w_pool, p_scale, g_post)


def _pool_sample_kernel(x_ref, pre_ref, gpre_ref, wpool_ref, pscale_ref, gpost_ref, o_ref, h_ref, *, n_b, t_s):
    x = x_ref[...]
    h = _rms(x, gpre_ref[...])
    h_ref[...] = h
    n_pre = W_MAX - 1
    rows = [pre_ref[i * n_b:(i + 1) * n_b, :] for i in range(n_pre)] + [h[t * n_b:(t + 1) * n_b, :] for t in range(t_s)]
    outs = []
    for t in range(t_s):
        ys = []
        for gi, w in enumerate(POOL_WINDOWS):
            sl = slice(gi * POOL_GC, (gi + 1) * POOL_GC)
            win = rows[n_pre + t][:, sl]
            for d in range(1, w):
                win = win + rows[n_pre + t - d][:, sl]
            pooled = win / float(w) - rows[n_pre + t][:, sl]
            ys.append(_dot(pooled.astype(BF16), wpool_ref[gi]))
        outs.append(jnp.concatenate(ys, axis=-1))
    y = jnp.concatenate(outs, axis=0) * pscale_ref[...]
    o_ref[...] = x + _rms(y, gpost_ref[...])


def _pool_sample(x, prefix, g_pre, w_pool, p_scale, g_post, n_b, t_s):
    args = (x, prefix, g_pre, w_pool, p_scale, g_post)
    return pl.pallas_call(
        functools.partial(_pool_sample_kernel, n_b=n_b, t_s=t_s),
        grid=(1,),
        in_specs=[_full(a.shape) for a in args],
        out_specs=[_full(x.shape), _full(x.shape)],
        out_shape=[jax.ShapeDtypeStruct(x.shape, F32)] * 2,
        compiler_params=_cparams(("arbitrary",)),
        name="pool_sample",
    )(*args)


def _rope_tables(pos):
    half = ROPE_DIM // 2
    inv = ROPE_BASE ** (-jnp.arange(half, dtype=F32) / half)
    ang = pos[:, None] * inv[None, :]
    cos, sin = jnp.cos(ang), jnp.sin(ang)
    reps = LANES // ROPE_DIM
    return jnp.tile(jnp.concatenate([cos, cos], -1), (1, reps)), jnp.tile(jnp.concatenate([-sin, sin], -1), (1, reps))


def _swap_halves(w):
    half = ROPE_DIM // 2
    return jnp.concatenate([w[..., half:], w[..., :half]], axis=-1)


def _ab_weights(e, w_in, g_q, w_uq, g_kv, w_uk):
    w = w_in[e]
    w_pe = w[:, Q_LORA + KV_LORA:MLA_IN]
    wq2 = jnp.concatenate([w[:, :Q_LORA], w_pe, _swap_halves(w_pe)], axis=1).astype(BF16)
    wkv = w[:, Q_LORA:Q_LORA + KV_LORA].astype(BF16)
    wb = w[:, MLA_IN:].astype(BF16)
    uq = w_uq[e].reshape(Q_LORA, H_A, NOPE_DIM + ROPE_DIM)
    pad = ((0, 0), (0, 0), (0, LANES - ROPE_DIM))
    uq_pe = uq[:, :, NOPE_DIM:]
    wuq = jnp.concatenate([uq[:, :, :NOPE_DIM].reshape(Q_LORA, H_A * NOPE_DIM),
                           jnp.pad(uq_pe, pad).reshape(Q_LORA, H_A * LANES),
                           jnp.pad(_swap_halves(uq_pe), pad).reshape(Q_LORA, H_A * LANES)], axis=1).astype(BF16)
    wuk = jnp.transpose(w_uk[e], (1, 2, 0)).astype(BF16)
    return wq2, wkv, wb, g_q[e][None], g_kv[e][None], wuq, wuk


def _rwkv_weights(e, mu_shift, w0, w2, a0, a2, g2, k_k, k_a, r_k):
    head = jnp.arange(D_B) // N_B
    ones = (head[:, None] == head[None, :]).astype(BF16)
    return (mu_shift[e][None], w0[e][None], w2[e].astype(BF16), a0[e][None], a2[e].astype(BF16), g2[e].astype(BF16),
            k_k[e][None], k_a[e][None], r_k[e].reshape(1, D_B), ones)


def kernel(x_prompt, x_sample, cache_ckv, cache_kpe, page_table, state_wkv, state_shift, state_pool, meta_tokens,
           g_mix_pre, g_mix_post, g_ffn_pre, g_ffn_post, w_in, g_q, w_uq, g_kv, w_uk, w_uv, mu_shift, w0, w2, a0, a2,
           g2, k_k, k_a, r_k, ln_w, ln_b, w_out, w_pool, pool_scale, w_up, w_down):
    n_bp, seq, _ = x_prompt.shape
    n_bs, t_s, _ = x_sample.shape
    depth = g_mix_pre.shape[0]
    t_p = seq + N_META
    n_pages = page_table.shape[1]
    past = n_pages * PAGE_SIZE
    m_p = n_bp * t_p
    m_s = n_bs * t_s

    meta = jnp.broadcast_to(meta_tokens[None].astype(x_prompt.dtype), (n_bp, N_META, D_MODEL))
    xp = jnp.concatenate([meta, x_prompt], axis=1).reshape(m_p, D_MODEL)
    xs = jnp.transpose(x_sample, (1, 0, 2)).reshape(m_s, D_MODEL)

    cos_p, sin_p = _rope_tables(jnp.tile(jnp.arange(t_p, dtype=F32), n_bp))
    cos_s, sin_s = _rope_tables(jnp.repeat(past + jnp.arange(t_s, dtype=F32), n_bs))
    t_pad = -(-t_p // LANES) * LANES
    tm_s = m_s

    ckv_p, kpe_p, wkv_p, shift_p, pool_p = [], [], [], [], []
    ckv_s, kpe_s, wkv_s, shift_s, pool_s = [], [], [], [], []
    for l in range(depth):
        g_pre, g_post = g_mix_pre[l][None], g_mix_post[l][None]
        if l % 2 == 0:
            e = l // 2
            ab_w = _ab_weights(e, w_in, g_q, w_uq, g_kv, w_uk)
            rw = _rwkv_weights(e, mu_shift, w0, w2, a0, a2, g2, k_k, k_a, r_k)
            wuv = jnp.transpose(w_uv[e], (1, 0, 2)).astype(BF16)
            wout = w_out[e].astype(BF16)
            lnw, lnb = ln_w[e][None], ln_b[e][None]

            ckv, kpe, zb, qabs = _ab_in_proj(xp, g_pre, ab_w, cos_p, sin_p, ROW_TILE)
            kv = jnp.concatenate([ckv, kpe], axis=-1).astype(BF16).reshape(n_bp, t_p, KV_LORA + ROPE_DIM)
            kv = jnp.pad(kv, ((0, 0), (0, t_pad - t_p), (0, QK_PAD - KV_LORA - ROPE_DIM)))
            olat = _attn_prompt(qabs, kv, t_p)
            zb3 = zb.reshape(n_bp, t_p, RWKV_IN)
            prev = jnp.pad(zb3[:, :-1], ((0, 0), (1, 0), (0, 0))).reshape(m_p, RWKV_IN)
            streams = _rwkv_prep(zb, prev, rw, ROW_TILE)
            o_b, s_fin = _rwkv_scan(streams, lnw, lnb, jnp.zeros((n_bp, H_B, N_B, N_B), F32), SCAN_CHUNK_P)
            xp = _ab_out_proj(olat, o_b, xp, wuv, wout, g_post, ROW_TILE)
            ckv_p.append(ckv.reshape(n_bp, t_p, KV_LORA))
            kpe_p.append(kpe.reshape(n_bp, t_p, ROPE_DIM))
            wkv_p.append(s_fin)
            shift_p.append(zb3[:, -1])

            ckv, kpe, zb, qabs = _ab_in_proj(xs, g_pre, ab_w, cos_s, sin_s, tm_s)
            q_s = jnp.transpose(qabs.reshape(H_A, t_s, n_bs, QK_PAD), (2, 0, 1, 3)).reshape(n_bs, H_A * t_s, QK_PAD)
            k_new = jnp.concatenate([ckv, kpe], axis=-1).astype(BF16).reshape(t_s, n_bs, KV_LORA + ROPE_DIM)
            k_new = jnp.pad(jnp.transpose(k_new, (1, 0, 2)),
                            ((0, 0), (0, SCAN_CHUNK_S - t_s), (0, QK_PAD - KV_LORA - ROPE_DIM)))
            o_s = _attn_sample(page_table, q_s, k_new, cache_ckv, cache_kpe, e, t_s)
            olat = jnp.transpose(o_s.reshape(n_bs, H_A, t_s, KV_LORA), (1, 2, 0, 3)).reshape(H_A, m_s, KV_LORA)
            prev = jnp.concatenate([state_shift[e].astype(F32), zb[:m_s - n_bs]], axis=0)
            streams = _rwkv_prep(zb, prev, rw, tm_s)
            pad_t = lambda a: jnp.pad(jnp.transpose(a.reshape(t_s, n_bs, D_B), (1, 0, 2)),
                                      ((0, 0), (0, SCAN_CHUNK_S - t_s), (0, 0))).reshape(n_bs * SCAN_CHUNK_S, D_B)
            o_b, s_fin = _rwkv_scan([pad_t(a) for a in streams], lnw, lnb, state_wkv[e].astype(F32), SCAN_CHUNK_S)
            o_b = jnp.transpose(o_b.reshape(n_bs, SCAN_CHUNK_S, D_B)[:, :t_s], (1, 0, 2)).reshape(m_s, D_B)
            xs = _ab_out_proj(olat, o_b, xs, wuv, wout, g_post, tm_s)
            ckv_s.append(jnp.transpose(ckv.reshape(t_s, n_bs, KV_LORA), (1, 0, 2)))
            kpe_s.append(jnp.transpose(kpe.reshape(t_s, n_bs, ROPE_DIM), (1, 0, 2)))
            wkv_s.append(s_fin)
            shift_s.append(zb[m_s - n_bs:])
        else:
            o = l // 2
            wp = w_pool[o].astype(BF16)
            ps = pool_scale[o][None]
            xp, tail = _pool_prompt(xp, g_pre, wp, ps, g_post, n_bp, t_p)
            pool_p.append(tail[:, 1:])
            prefix = jnp.transpose(state_pool[o].astype(F32), (1, 0, 2))
            xs, h_s = _pool_sample(xs, prefix.reshape((W_MAX - 1) * n_bs, D_MODEL), g_pre, wp, ps, g_post, n_bs, t_s)
            full = jnp.concatenate([prefix, h_s.reshape(t_s, n_bs, D_MODEL)], axis=0)
            pool_s.append(jnp.transpose(full[-(W_MAX - 1):], (1, 0, 2)))
        wu, wd = w_up[l].astype(BF16), w_down[l].astype(BF16)
        gfp, gfo = g_ffn_pre[l][None], g_ffn_post[l][None]
        xp = _ffn(xp, gfp, wu, wd, gfo, ROW_TILE)
        xs = _ffn(xs, gfp, wu, wd, gfo, tm_s)

    y_prompt = xp.reshape(n_bp, t_p, D_MODEL)[:, N_META:]
    y_sample = jnp.transpose(xs.reshape(t_s, n_bs, D_MODEL), (1, 0, 2))
    return (y_prompt, y_sample, jnp.stack(ckv_p), jnp.stack(kpe_p), jnp.stack(wkv_p), jnp.stack(shift_p),
            jnp.stack(pool_p), jnp.stack(ckv_s), jnp.stack(kpe_s), jnp.stack(wkv_s), jnp.stack(shift_s),
            jnp.stack(pool_s))
```

```python
import functools

import jax
import jax.numpy as jnp
from jax import lax
from jax.experimental import pallas as pl
from jax.experimental.pallas import tpu as pltpu

F32 = jnp.float32
BF16 = jnp.bfloat16

D_MODEL = 1024
N_META = 16
PAGE_SIZE = 128
H_A = 8
Q_LORA = 384
KV_LORA = 256
NOPE_DIM = 64
ROPE_DIM = 32
V_DIM = 64
ROPE_BASE = 10000.0
SM_SCALE = (NOPE_DIM + ROPE_DIM) ** -0.5
H_B = 8
N_B = 64
D_B = H_B * N_B
DECAY_LORA = 64
AAA_LORA = 64
GATE_LORA = 160
RWKV_IN = 3 * D_B + DECAY_LORA + AAA_LORA + GATE_LORA
MLA_IN = Q_LORA + KV_LORA + ROPE_DIM
GN_EPS = 64e-5
POOL_WINDOWS = (2, 4, 8, 16)
POOL_GC = D_MODEL // len(POOL_WINDOWS)
W_MAX = 16
D_FF = 4 * D_MODEL
RMS_EPS = 1e-6

LANES = 128
QK_PAD = KV_LORA + LANES
VMEM_LIMIT = 56 * 1024 * 1024
ROW_TILE = 384
ATT_Q_TILE = 688
POOL_TILE = 688
SCAN_CHUNK_P = 48
SCAN_CHUNK_S = 8
SCAN_SEQS_P = 2
SCAN_SEQS_S = 4
NEW_KEY_ROWS = 16
PAGES_PER_CHUNK = 8


def _cparams(sem):
    return pltpu.CompilerParams(dimension_semantics=sem, vmem_limit_bytes=VMEM_LIMIT)


def _rms(x, g):
    return x * lax.rsqrt(jnp.mean(x * x, axis=-1, keepdims=True) + RMS_EPS) * g


def _dot(a, b):
    return jnp.dot(a, b, preferred_element_type=F32)


def _dot_nt(a, b):
    return lax.dot_general(a, b, (((1,), (1,)), ((), ())), preferred_element_type=F32)


def _dot_tn(a, b):
    return lax.dot_general(a, b, (((0,), (0,)), ((), ())), preferred_element_type=F32)


def _full(shape):
    n = len(shape)
    return pl.BlockSpec(shape, lambda *_: (0,) * n)


def _ab_in_kernel(x_ref, gpre_ref, wq2_ref, wkv_ref, wb_ref, gq_ref, gkv_ref, wuq_ref, wuk_ref, cos_ref, sin_ref,
                  ckv_ref, kpe_ref, zb_ref, qabs_ref):
    h = _rms(x_ref[...], gpre_ref[...]).astype(BF16)
    zq2 = _dot(h, wq2_ref[...])
    zkv = _dot(h, wkv_ref[...])
    zb_ref[...] = _dot(h, wb_ref[...])
    ckv_ref[...] = _rms(zkv, gkv_ref[...])
    cos = cos_ref[...]
    sin = sin_ref[...]
    pe = zq2[:, Q_LORA:Q_LORA + ROPE_DIM]
    pe_sw = zq2[:, Q_LORA + ROPE_DIM:Q_LORA + 2 * ROPE_DIM]
    kpe_ref[...] = pe * cos[:, :ROPE_DIM] + pe_sw * sin[:, :ROPE_DIM]
    qn = _rms(zq2[:, :Q_LORA], gq_ref[...]).astype(BF16)
    q2 = _dot(qn, wuq_ref[...])
    n_nope = H_A * NOPE_DIM
    for hd in range(H_A):
        qn_h = q2[:, hd * NOPE_DIM:(hd + 1) * NOPE_DIM].astype(BF16)
        qabs_ref[hd, :, 0:KV_LORA] = _dot(qn_h, wuk_ref[hd]).astype(BF16)
        r0 = n_nope + hd * LANES
        r1 = n_nope + (H_A + hd) * LANES
        qabs_ref[hd, :, KV_LORA:QK_PAD] = (q2[:, r0:r0 + LANES] * cos + q2[:, r1:r1 + LANES] * sin).astype(BF16)


def _ab_in_proj(x, g_pre, wts, cos_t, sin_t, tm):
    m = x.shape[0]
    wq2, wkv, wb, g_q, g_kv, wuq, wuk = wts
    row = lambda w: pl.BlockSpec((tm, w), lambda i: (i, 0))
    return pl.pallas_call(
        _ab_in_kernel,
        grid=(m // tm,),
        in_specs=[row(D_MODEL), _full(g_pre.shape), _full(wq2.shape), _full(wkv.shape), _full(wb.shape),
                  _full(g_q.shape), _full(g_kv.shape), _full(wuq.shape), _full(wuk.shape), row(LANES), row(LANES)],
        out_specs=[row(KV_LORA), row(ROPE_DIM), row(RWKV_IN),
                   pl.BlockSpec((H_A, tm, QK_PAD), lambda i: (0, i, 0))],
        out_shape=[jax.ShapeDtypeStruct((m, KV_LORA), F32), jax.ShapeDtypeStruct((m, ROPE_DIM), F32),
                   jax.ShapeDtypeStruct((m, RWKV_IN), F32), jax.ShapeDtypeStruct((H_A, m, QK_PAD), BF16)],
        compiler_params=_cparams(("parallel",)),
        name="ab_in_proj",
    )(x, g_pre, wq2, wkv, wb, g_q, g_kv, wuq, wuk, cos_t, sin_t)


def _attn_prompt_kernel(q_ref, kv_ref, o_ref, *, t_p, tq):
    for i in range(t_p // tq):
        kext = min(-(-((i + 1) * tq) // LANES) * LANES, kv_ref.shape[1])
        q = q_ref[0, i * tq:(i + 1) * tq, :]
        k = kv_ref[0, :kext, :]
        s = _dot_nt(q, k)
        q_pos = i * tq + lax.broadcasted_iota(jnp.int32, (tq, kext), 0)
        k_pos = lax.broadcasted_iota(jnp.int32, (tq, kext), 1)
        s = jnp.where(k_pos <= q_pos, s * SM_SCALE, -jnp.inf)
        p = jnp.exp(s - jnp.max(s, axis=-1, keepdims=True))
        l = jnp.sum(p, axis=-1, keepdims=True)
        o = _dot(p.astype(BF16), k[:, :KV_LORA])
        o_ref[0, i * tq:(i + 1) * tq, :] = (o / l).astype(BF16)


def _attn_prompt(qabs, kv, t_p):
    n_b, t_pad = kv.shape[0], kv.shape[1]
    m = qabs.shape[1]
    return pl.pallas_call(
        functools.partial(_attn_prompt_kernel, t_p=t_p, tq=ATT_Q_TILE),
        grid=(n_b, H_A),
        in_specs=[pl.BlockSpec((1, t_p, QK_PAD), lambda b, h: (h, b, 0)),
                  pl.BlockSpec((1, t_pad, QK_PAD), lambda b, h: (b, 0, 0))],
        out_specs=pl.BlockSpec((1, t_p, KV_LORA), lambda b, h: (h, b, 0)),
        out_shape=jax.ShapeDtypeStruct((H_A, m, KV_LORA), BF16),
        compiler_params=_cparams(("parallel", "parallel")),
        name="attn_prompt",
    )(qabs, kv)


def _attn_sample_kernel(pt_ref, q_ref, knew_ref, ckv_hbm, kpe_hbm, o_ref, ckv_buf, kpe_buf, s_scr, sems, *, layer,
                        n_pages, t_s):
    b = pl.program_id(0)
    n_b = pl.num_programs(0)
    slot = b % 2

    def page_copies(bb, sl, p):
        page = pt_ref[bb, p]
        return (pltpu.make_async_copy(ckv_hbm.at[layer, page], ckv_buf.at[sl, p], sems.at[0, sl]),
                pltpu.make_async_copy(kpe_hbm.at[layer, page], kpe_buf.at[sl, p], sems.at[1, sl]))

    def start_all(bb, sl):
        for p in range(n_pages):
            for cp in page_copies(bb, sl, p):
                cp.start()

    @pl.when(b == 0)
    def _():
        start_all(0, 0)

    @pl.when(b + 1 < n_b)
    def _():
        start_all(b + 1, 1 - slot)

    for p in range(n_pages):
        for cp in page_copies(b, slot, p):
            cp.wait()

    q = q_ref[0]
    q_lat = q[:, :KV_LORA]
    q_pe = q[:, KV_LORA:KV_LORA + ROPE_DIM]
    rows = PAGES_PER_CHUNK * PAGE_SIZE
    n_chunks = n_pages // PAGES_PER_CHUNK

    def latent_rows(c):
        return ckv_buf[slot, c * PAGES_PER_CHUNK:(c + 1) * PAGES_PER_CHUNK].reshape(rows, KV_LORA).astype(BF16)

    for c in range(n_chunks):
        kp = jnp.concatenate([kpe_buf[slot, c * PAGES_PER_CHUNK + i] for i in range(PAGES_PER_CHUNK)], axis=1)
        s_scr[:, c * rows:(c + 1) * rows] = (_dot_nt(q_lat, latent_rows(c)) + _dot(q_pe, kp.astype(BF16))) * SM_SCALE
    kn = knew_ref[0]
    s_new = _dot_nt(q, kn) * SM_SCALE
    step = lax.broadcasted_iota(jnp.int32, s_new.shape, 0) % t_s
    key = lax.broadcasted_iota(jnp.int32, s_new.shape, 1)
    s_new = jnp.where(key <= step, s_new, -jnp.inf)
    mx = s_scr[:, 0:rows]
    for c in range(1, n_chunks):
        mx = jnp.maximum(mx, s_scr[:, c * rows:(c + 1) * rows])
    m = jnp.maximum(jnp.max(mx, axis=-1, keepdims=True), jnp.max(s_new, axis=-1, keepdims=True))
    p_new = jnp.exp(s_new - m)
    acc = _dot(p_new.astype(BF16), kn[:, :KV_LORA])
    p_sum = jnp.zeros((q.shape[0], rows), F32)
    for c in range(n_chunks):
        p = jnp.exp(s_scr[:, c * rows:(c + 1) * rows] - m)
        p_sum = p_sum + p
        acc = acc + _dot(p.astype(BF16), latent_rows(c))
    l = jnp.sum(p_sum, axis=-1, keepdims=True) + jnp.sum(p_new, axis=-1, keepdims=True)
    o_ref[0] = (acc / l).astype(BF16)


def _attn_sample(page_table, q_s, k_new, cache_ckv, cache_kpe, layer, t_s):
    n_b, n_pages = page_table.shape
    n_q = q_s.shape[1]
    grid_spec = pltpu.PrefetchScalarGridSpec(
        num_scalar_prefetch=1,
        grid=(n_b,),
        in_specs=[pl.BlockSpec((1, n_q, QK_PAD), lambda b, pt: (b, 0, 0)),
                  pl.BlockSpec((1,) + k_new.shape[1:], lambda b, pt: (b, 0, 0)),
                  pl.BlockSpec(memory_space=pl.ANY),
                  pl.BlockSpec(memory_space=pl.ANY)],
        out_specs=pl.BlockSpec((1, n_q, KV_LORA), lambda b, pt: (b, 0, 0)),
        scratch_shapes=[pltpu.VMEM((2, n_pages, PAGE_SIZE, KV_LORA), F32),
                        pltpu.VMEM((2, n_pages, ROPE_DIM, PAGE_SIZE), F32),
                        pltpu.VMEM((n_q, n_pages * PAGE_SIZE), F32),
                        pltpu.SemaphoreType.DMA((2, 2))],
    )
    return pl.pallas_call(
        functools.partial(_attn_sample_kernel, layer=layer, n_pages=n_pages, t_s=t_s),
        grid_spec=grid_spec,
        out_shape=jax.ShapeDtypeStruct((n_b, n_q, KV_LORA), BF16),
        compiler_params=_cparams(("arbitrary",)),
        name="attn_sample",
    )(page_table, q_s, k_new, cache_ckv, cache_kpe)


def _ab_out_kernel(olat_ref, ob_ref, x_ref, wuv_ref, wout_ref, gpost_ref, o_ref):
    n_a = H_A * V_DIM
    y = _dot(ob_ref[...].astype(BF16), wout_ref[n_a:, :])
    for hd in range(H_A):
        oa = _dot(olat_ref[hd], wuv_ref[hd]).astype(BF16)
        y = y + _dot(oa, wout_ref[hd * V_DIM:(hd + 1) * V_DIM, :])
    o_ref[...] = x_ref[...] + _rms(y, gpost_ref[...])


def _ab_out_proj(olat, o_b, x, wuv, wout, g_post, tm):
    m = x.shape[0]
    row = lambda w: pl.BlockSpec((tm, w), lambda i: (i, 0))
    return pl.pallas_call(
        _ab_out_kernel,
        grid=(m // tm,),
        in_specs=[pl.BlockSpec((H_A, tm, KV_LORA), lambda i: (0, i, 0)), row(D_B), row(D_MODEL),
                  _full(wuv.shape), _full(wout.shape), _full(g_post.shape)],
        out_specs=row(D_MODEL),
        out_shape=jax.ShapeDtypeStruct((m, D_MODEL), F32),
        compiler_params=_cparams(("parallel",)),
        name="ab_out_proj",
    )(olat, o_b, x, wuv, wout, g_post)


def _ffn_kernel(x_ref, gpre_ref, wup_ref, wdown_ref, gpost_ref, o_ref, *, ff_chunk):
    x = x_ref[...]
    h = _rms(x, gpre_ref[...]).astype(BF16)
    y = jnp.zeros(x.shape, F32)
    for c in range(D_FF // ff_chunk):
        u = _dot(h, wup_ref[:, c * ff_chunk:(c + 1) * ff_chunk])
        a = jnp.square(jnp.maximum(u, 0.0)).astype(BF16)
        y = y + _dot(a, wdown_ref[c * ff_chunk:(c + 1) * ff_chunk, :])
    o_ref[...] = x + _rms(y, gpost_ref[...])


def _ffn(x, g_pre, w_up, w_down, g_post, tm):
    m = x.shape[0]
    row = pl.BlockSpec((tm, D_MODEL), lambda i: (i, 0))
    return pl.pallas_call(
        functools.partial(_ffn_kernel, ff_chunk=1024),
        grid=(m // tm,),
        in_specs=[row, _full(g_pre.shape), _full(w_up.shape), _full(w_down.shape), _full(g_post.shape)],
        out_specs=row,
        out_shape=jax.ShapeDtypeStruct((m, D_MODEL), F32),
        compiler_params=_cparams(("parallel",)),
        name="ffn",
    )(x, g_pre, w_up, w_down, g_post)


def _split_dot(x, w):
    hi = x.astype(BF16)
    lo = (x - hi.astype(F32)).astype(BF16)
    return _dot(hi, w) + _dot(lo, w)


def _rwkv_prep_kernel(zb_ref, prev_ref, mu_ref, w0_ref, w2_ref, a0_ref, a2_ref, g2_ref, kk_ref, ka_ref, rk_ref,
                      ones_ref, r_o, lw_o, keff_o, v_o, kn_o, b_o, g_o, bonus_o):
    zb = zb_ref[...]
    xm = zb + (prev_ref[...] - zb) * mu_ref[...]
    o3 = 3 * D_B
    o4 = o3 + DECAY_LORA
    o5 = o4 + AAA_LORA
    r = xm[:, :D_B]
    k = xm[:, D_B:2 * D_B]
    v = xm[:, 2 * D_B:o3]
    z = -(w0_ref[...] + _dot(jnp.tanh(xm[:, o3:o4]).astype(BF16), w2_ref[...]))
    softplus = jnp.maximum(z, 0.0) + jnp.log(1.0 + jnp.exp(-jnp.abs(z)))
    lw_o[...] = -jnp.exp(-softplus - 0.5)
    a = 1.0 / (1.0 + jnp.exp(-(a0_ref[...] + _dot(xm[:, o4:o5].astype(BF16), a2_ref[...]))))
    sg = 1.0 / (1.0 + jnp.exp(-xm[:, o5:]))
    g_o[...] = _dot(sg.astype(BF16), g2_ref[...])
    ones = ones_ref[...]
    kk = k * kk_ref[...]
    kk = kk / jnp.maximum(jnp.sqrt(_split_dot(kk * kk, ones)), 1e-12)
    keff = k * (1.0 + (a - 1.0) * ka_ref[...])
    r_o[...] = r
    keff_o[...] = keff
    v_o[...] = v
    kn_o[...] = kk
    b_o[...] = kk * a
    bonus_o[...] = _split_dot(r * keff * rk_ref[...], ones) * v


def _rwkv_prep(zb, prev, rw, tm):
    m = zb.shape[0]
    row = lambda w: pl.BlockSpec((tm, w), lambda i: (i, 0))
    return pl.pallas_call(
        _rwkv_prep_kernel,
        grid=(m // tm,),
        in_specs=[row(RWKV_IN), row(RWKV_IN)] + [_full(w.shape) for w in rw],
        out_specs=[row(D_B)] * 8,
        out_shape=[jax.ShapeDtypeStruct((m, D_B), F32)] * 8,
        compiler_params=_cparams(("parallel",)),
        name="rwkv_prep",
    )(zb, prev, *rw)


def _rwkv_scan_kernel(r_ref, lw_ref, keff_ref, v_ref, kn_ref, b_ref, g_ref, bonus_ref, lnw_ref, lnb_ref, s0_ref,
                      tri_ref, o_ref, s_out_ref, s_scr, *, chunk, n_sb):
    c = pl.program_id(1)

    @pl.when(c == 0)
    def _():
        s_scr[...] = s0_ref[...]

    t_i = lax.broadcasted_iota(jnp.int32, (chunk, chunk), 0)
    s_i = lax.broadcasted_iota(jnp.int32, (chunk, chunk), 1)
    strict = s_i < t_i
    incl = s_i <= t_i
    eye = jnp.where(s_i == t_i, 1.0, 0.0).astype(F32)
    levels = []
    k_lvl = 1
    while (1 << (k_lvl - 1)) < chunk:
        same_new = (t_i >> k_lvl) == (s_i >> k_lvl)
        same_old = (t_i >> (k_lvl - 1)) == (s_i >> (k_lvl - 1))
        levels.append(strict & same_new & jnp.logical_not(same_old))
        k_lvl += 1

    tri = tri_ref[...]
    chains = [(j, hd) for j in range(n_sb) for hd in range(H_B)]
    qr, bk, kk, bk_e, kk_e, v, kq32, rq32, g_last = {}, {}, {}, {}, {}, {}, {}, {}, {}
    for j in range(n_sb):
        lw = lw_ref[j]
        hi = lw.astype(BF16)
        r1 = lw - hi.astype(F32)
        mid = r1.astype(BF16)
        lo = (r1 - mid.astype(F32)).astype(BF16)
        cs = _dot(tri, hi) + _dot(tri, mid) + _dot(tri, lo)
        cs_last = cs[chunk - 1:chunk, :]
        g_inv = jnp.exp(-cs)
        g_end = jnp.exp(cs_last - cs)
        kq_j = kn_ref[j] * jnp.exp(cs - lw)
        rq_j = r_ref[j] * jnp.exp(cs)
        qr_j = jnp.concatenate([kq_j, rq_j], axis=0).astype(BF16)
        b_j = b_ref[j]
        ke_j = keff_ref[j]
        bk_j = (b_j * g_inv).astype(BF16)
        kk_j = (ke_j * g_inv).astype(BF16)
        bke_j = (b_j * g_end).astype(BF16)
        kke_j = (ke_j * g_end).astype(BF16)
        v_j = v_ref[j].astype(BF16)
        gl_j = jnp.exp(cs_last)
        for hd in range(H_B):
            sl = slice(hd * N_B, (hd + 1) * N_B)
            ch = (j, hd)
            qr[ch], bk[ch], kk[ch], bk_e[ch], kk_e[ch], v[ch] = (qr_j[:, sl], bk_j[:, sl], kk_j[:, sl], bke_j[:, sl],
                                                                 kke_j[:, sl], v_j[:, sl])
            kq32[ch], rq32[ch], g_last[ch] = kq_j[:, sl], rq_j[:, sl], gl_j[:, sl]

    p1 = {ch: _dot_nt(qr[ch], bk[ch]) for ch in chains}
    p2 = {ch: _dot_nt(qr[ch], kk[ch]) for ch in chains}
    a_m = {ch: -p1[ch][:chunk] for ch in chains}
    t_m = {ch: eye + jnp.where(levels[0], a_m[ch], 0.0) for ch in chains}
    w_m = {ch: _dot(jnp.where(strict, p2[ch][:chunk], 0.0).astype(BF16), v[ch]) for ch in chains}
    for lvl in levels[1:]:
        x_m = {ch: _dot(jnp.where(lvl, a_m[ch], 0.0).astype(BF16), t_m[ch].astype(BF16)) for ch in chains}
        t_m = {ch: t_m[ch] + _dot(t_m[ch].astype(BF16), x_m[ch].astype(BF16)) for ch in chains}
    z_m = {ch: _dot(t_m[ch].astype(BF16), jnp.concatenate([kq32[ch], w_m[ch]], axis=1).astype(BF16))
           for ch in chains}
    z_b = {ch: z_m[ch].astype(BF16) for ch in chains}
    d1 = {ch: _dot(jnp.where(incl, -p1[ch][chunk:], 0.0).astype(BF16), z_b[ch]) for ch in chains}
    d2 = {ch: _dot(jnp.where(incl, p2[ch][chunk:], 0.0).astype(BF16), v[ch]) for ch in chains}
    mn = {ch: _dot_tn(z_b[ch], bk_e[ch]) for ch in chains}
    vk = {ch: _dot_tn(v[ch], kk_e[ch]) for ch in chains}

    s_old = {ch: s_scr[ch[0], ch[1]] for ch in chains}
    s_b = {ch: s_old[ch].astype(BF16) for ch in chains}
    y = {ch: _dot_nt((rq32[ch] + d1[ch][:, :N_B]).astype(BF16), s_b[ch]) + d1[ch][:, N_B:] + d2[ch] for ch in chains}
    s_new = {ch: s_old[ch] * g_last[ch] - _dot(s_b[ch], mn[ch][:N_B].astype(BF16)) + (vk[ch] - mn[ch][N_B:])
             for ch in chains}
    for ch in chains:
        s_scr[ch[0], ch[1]] = s_new[ch]
        s_out_ref[ch[0], ch[1]] = s_new[ch]
    for j in range(n_sb):
        outs = []
        for hd in range(H_B):
            y_h = y[(j, hd)]
            mean = jnp.mean(y_h, axis=-1, keepdims=True)
            var = jnp.mean(jnp.square(y_h - mean), axis=-1, keepdims=True)
            outs.append((y_h - mean) * lax.rsqrt(var + GN_EPS))
        yn = jnp.concatenate(outs, axis=-1) * lnw_ref[...] + lnb_ref[...]
        o_ref[j] = (yn + bonus_ref[j]) * g_ref[j]


def _rwkv_scan(streams, ln_w, ln_b, s0, chunk, n_sb):
    n_seq, t, _ = streams[0].shape
    n_chunks = t // chunk
    tri = jnp.tril(jnp.ones((chunk, chunk), F32)).astype(BF16)
    row = pl.BlockSpec((n_sb, chunk, D_B), lambda b, c: (b, c, 0))
    st = pl.BlockSpec((n_sb, H_B, N_B, N_B), lambda b, c: (b, 0, 0, 0))
    return pl.pallas_call(
        functools.partial(_rwkv_scan_kernel, chunk=chunk, n_sb=n_sb),
        grid=(n_seq // n_sb, n_chunks),
        in_specs=[row] * 8 + [_full(ln_w.shape), _full(ln_b.shape), st, _full(tri.shape)],
        out_specs=[row, st],
        out_shape=[jax.ShapeDtypeStruct((n_seq, t, D_B), F32), jax.ShapeDtypeStruct(s0.shape, F32)],
        scratch_shapes=[pltpu.VMEM((n_sb, H_B, N_B, N_B), F32)],
        compiler_params=_cparams(("parallel", "arbitrary")),
        name="rwkv_scan",
    )(*streams, ln_w, ln_b, s0, tri)


def _pool_prompt_kernel(x_ref, gpre_ref, wpool_ref, pscale_ref, gpost_ref, o_ref, tail_ref, hext, *, tm):
    j = pl.program_id(1)
    x = x_ref[...]
    h = _rms(x, gpre_ref[...])

    @pl.when(j == 0)
    def _():
        hext[0:W_MAX, :] = jnp.zeros((W_MAX, D_MODEL), F32)

    @pl.when(j > 0)
    def _():
        hext[0:W_MAX, :] = hext[tm:tm + W_MAX, :]

    hext[W_MAX:, :] = h
    tail_ref[0] = h[tm - W_MAX:, :]
    pos = j * tm + lax.broadcasted_iota(jnp.int32, (tm, 1), 0)
    acc = hext[...]
    ys = []
    for gi, w in enumerate(POOL_WINDOWS):
        acc = acc[:, POOL_GC * (1 if gi else 0):]
        acc = acc + pltpu.roll(acc, w // 2, 0)
        cnt = jnp.minimum(pos + 1, w).astype(F32)
        pooled = acc[W_MAX:, :POOL_GC] / cnt - h[:, gi * POOL_GC:(gi + 1) * POOL_GC]
        ys.append(_dot(pooled.astype(BF16), wpool_ref[gi]))
    y = jnp.concatenate(ys, axis=-1) * pscale_ref[...]
    o_ref[...] = x + _rms(y, gpost_ref[...])


def _pool_prompt(x, g_pre, w_pool, p_scale, g_post, n_b, t_p):
    m = x.shape[0]
    tm = POOL_TILE
    n_t = t_p // tm
    row = pl.BlockSpec((tm, D_MODEL), lambda b, j: (b * n_t + j, 0))
    return pl.pallas_call(
        functools.partial(_pool_prompt_kernel, tm=tm),
        grid=(n_b, n_t),
        in_specs=[row, _full(g_pre.shape), _full(w_pool.shape), _full(p_scale.shape), _full(g_post.shape)],
        out_specs=[row, pl.BlockSpec((1, W_MAX, D_MODEL), lambda b, j: (b, 0, 0))],
        out_shape=[jax.ShapeDtypeStruct((m, D_MODEL), F32), jax.ShapeDtypeStruct((n_b, W_MAX, D_MODEL), F32)],
        scratch_shapes=[pltpu.VMEM((W_MAX + tm, D_MODEL), F32)],
        compiler_params=_cparams(("parallel", "arbitrary")),
        name="pool_prompt",
    )(x, g_pre, w_pool, p_scale, g_post)


def _pool_sample_kernel(x_ref, pre_ref, gpre_ref, wpool_ref, pscale_ref, gpost_ref, o_ref, h_ref, *, n_b, t_s):
    x = x_ref[...]
    h = _rms(x, gpre_ref[...])
    h_ref[...] = h
    n_pre = W_MAX - 1
    rows = [pre_ref[i * n_b:(i + 1) * n_b, :] for i in range(n_pre)] + [h[t * n_b:(t + 1) * n_b, :] for t in range(t_s)]
    outs = []
    for t in range(t_s):
        ys = []
        for gi, w in enumerate(POOL_WINDOWS):
            sl = slice(gi * POOL_GC, (gi + 1) * POOL_GC)
            win = rows[n_pre + t][:, sl]
            for d in range(1, w):
                win = win + rows[n_pre + t - d][:, sl]
            pooled = win / float(w) - rows[n_pre + t][:, sl]
            ys.append(_dot(pooled.astype(BF16), wpool_ref[gi]))
        outs.append(jnp.concatenate(ys, axis=-1))
    y = jnp.concatenate(outs, axis=0) * pscale_ref[...]
    o_ref[...] = x + _rms(y, gpost_ref[...])


def _pool_sample(x, prefix, g_pre, w_pool, p_scale, g_post, n_b, t_s):
    args = (x, prefix, g_pre, w_pool, p_scale, g_post)
    return pl.pallas_call(
        functools.partial(_pool_sample_kernel, n_b=n_b, t_s=t_s),
        grid=(1,),
        in_specs=[_full(a.shape) for a in args],
        out_specs=[_full(x.shape), _full(x.shape)],
        out_shape=[jax.ShapeDtypeStruct(x.shape, F32)] * 2,
        compiler_params=_cparams(("arbitrary",)),
        name="pool_sample",
    )(*args)


def _rope_tables(pos):
    half = ROPE_DIM // 2
    inv = ROPE_BASE ** (-jnp.arange(half, dtype=F32) / half)
    ang = pos[:, None] * inv[None, :]
    cos, sin = jnp.cos(ang), jnp.sin(ang)
    reps = LANES // ROPE_DIM
    return jnp.tile(jnp.concatenate([cos, cos], -1), (1, reps)), jnp.tile(jnp.concatenate([-sin, sin], -1), (1, reps))


def _swap_halves(w):
    half = ROPE_DIM // 2
    return jnp.concatenate([w[..., half:], w[..., :half]], axis=-1)


def _ab_weights(e, w_in, g_q, w_uq, g_kv, w_uk):
    w = w_in[e]
    w_pe = w[:, Q_LORA + KV_LORA:MLA_IN]
    wq2 = jnp.concatenate([w[:, :Q_LORA], w_pe, _swap_halves(w_pe)], axis=1).astype(BF16)
    wkv = w[:, Q_LORA:Q_LORA + KV_LORA].astype(BF16)
    wb = w[:, MLA_IN:].astype(BF16)
    uq = w_uq[e].reshape(Q_LORA, H_A, NOPE_DIM + ROPE_DIM)
    pad = ((0, 0), (0, 0), (0, LANES - ROPE_DIM))
    uq_pe = uq[:, :, NOPE_DIM:]
    wuq = jnp.concatenate([uq[:, :, :NOPE_DIM].reshape(Q_LORA, H_A * NOPE_DIM),
                           jnp.pad(uq_pe, pad).reshape(Q_LORA, H_A * LANES),
                           jnp.pad(_swap_halves(uq_pe), pad).reshape(Q_LORA, H_A * LANES)], axis=1).astype(BF16)
    wuk = jnp.transpose(w_uk[e], (1, 2, 0)).astype(BF16)
    return wq2, wkv, wb, g_q[e][None], g_kv[e][None], wuq, wuk


def _rwkv_weights(e, mu_shift, w0, w2, a0, a2, g2, k_k, k_a, r_k):
    head = jnp.arange(D_B) // N_B
    ones = (head[:, None] == head[None, :]).astype(BF16)
    return (mu_shift[e][None], w0[e][None], w2[e].astype(BF16), a0[e][None], a2[e].astype(BF16), g2[e].astype(BF16),
            k_k[e][None], k_a[e][None], r_k[e].reshape(1, D_B), ones)


def kernel(x_prompt, x_sample, cache_ckv, cache_kpe, page_table, state_wkv, state_shift, state_pool, meta_tokens,
           g_mix_pre, g_mix_post, g_ffn_pre, g_ffn_post, w_in, g_q, w_uq, g_kv, w_uk, w_uv, mu_shift, w0, w2, a0, a2,
           g2, k_k, k_a, r_k, ln_w, ln_b, w_out, w_pool, pool_scale, w_up, w_down):
    n_bp, seq, _ = x_prompt.shape
    n_bs, t_s, _ = x_sample.shape
    depth = g_mix_pre.shape[0]
    t_p = seq + N_META
    n_pages = page_table.shape[1]
    past = n_pages * PAGE_SIZE
    m_p = n_bp * t_p
    m_s = n_bs * t_s

    meta = jnp.broadcast_to(meta_tokens[None].astype(x_prompt.dtype), (n_bp, N_META, D_MODEL))
    xp = jnp.concatenate([meta, x_prompt], axis=1).reshape(m_p, D_MODEL)
    xs = jnp.transpose(x_sample, (1, 0, 2)).reshape(m_s, D_MODEL)

    cos_p, sin_p = _rope_tables(jnp.tile(jnp.arange(t_p, dtype=F32), n_bp))
    cos_s, sin_s = _rope_tables(jnp.repeat(past + jnp.arange(t_s, dtype=F32), n_bs))
    t_pad = -(-t_p // LANES) * LANES
    tm_s = m_s
    cache_kpe_t = jnp.swapaxes(cache_kpe, 2, 3)

    ckv_p, kpe_p, wkv_p, shift_p, pool_p = [], [], [], [], []
    ckv_s, kpe_s, wkv_s, shift_s, pool_s = [], [], [], [], []
    for l in range(depth):
        g_pre, g_post = g_mix_pre[l][None], g_mix_post[l][None]
        if l % 2 == 0:
            e = l // 2
            ab_w = _ab_weights(e, w_in, g_q, w_uq, g_kv, w_uk)
            rw = _rwkv_weights(e, mu_shift, w0, w2, a0, a2, g2, k_k, k_a, r_k)
            wuv = jnp.transpose(w_uv[e], (1, 0, 2)).astype(BF16)
            wout = w_out[e].astype(BF16)
            lnw, lnb = ln_w[e][None], ln_b[e][None]

            ckv, kpe, zb, qabs = _ab_in_proj(xp, g_pre, ab_w, cos_p, sin_p, ROW_TILE)
            kv = jnp.concatenate([ckv, kpe], axis=-1).astype(BF16).reshape(n_bp, t_p, KV_LORA + ROPE_DIM)
            kv = jnp.pad(kv, ((0, 0), (0, t_pad - t_p), (0, QK_PAD - KV_LORA - ROPE_DIM)))
            olat = _attn_prompt(qabs, kv, t_p)
            zb3 = zb.reshape(n_bp, t_p, RWKV_IN)
            prev = jnp.pad(zb3[:, :-1], ((0, 0), (1, 0), (0, 0))).reshape(m_p, RWKV_IN)
            streams = [a.reshape(n_bp, t_p, D_B) for a in _rwkv_prep(zb, prev, rw, ROW_TILE)]
            o_b, s_fin = _rwkv_scan(streams, lnw, lnb, jnp.zeros((n_bp, H_B, N_B, N_B), F32), SCAN_CHUNK_P,
                                    SCAN_SEQS_P)
            xp = _ab_out_proj(olat, o_b.reshape(m_p, D_B), xp, wuv, wout, g_post, ROW_TILE)
            ckv_p.append(ckv.reshape(n_bp, t_p, KV_LORA))
            kpe_p.append(kpe.reshape(n_bp, t_p, ROPE_DIM))
            wkv_p.append(s_fin)
            shift_p.append(zb3[:, -1])

            ckv, kpe, zb, qabs = _ab_in_proj(xs, g_pre, ab_w, cos_s, sin_s, tm_s)
            q_s = jnp.transpose(qabs.reshape(H_A, t_s, n_bs, QK_PAD), (2, 0, 1, 3)).reshape(n_bs, H_A * t_s, QK_PAD)
            k_new = jnp.concatenate([ckv, kpe], axis=-1).astype(BF16).reshape(t_s, n_bs, KV_LORA + ROPE_DIM)
            k_new = jnp.pad(jnp.transpose(k_new, (1, 0, 2)),
                            ((0, 0), (0, NEW_KEY_ROWS - t_s), (0, QK_PAD - KV_LORA - ROPE_DIM)))
            o_s = _attn_sample(page_table, q_s, k_new, cache_ckv, cache_kpe_t, e, t_s)
            olat = jnp.transpose(o_s.reshape(n_bs, H_A, t_s, KV_LORA), (1, 2, 0, 3)).reshape(H_A, m_s, KV_LORA)
            prev = jnp.concatenate([state_shift[e].astype(F32), zb[:m_s - n_bs]], axis=0)
            streams = _rwkv_prep(zb, prev, rw, tm_s)
            pad_t = lambda a: jnp.pad(jnp.transpose(a.reshape(t_s, n_bs, D_B), (1, 0, 2)),
                                      ((0, 0), (0, SCAN_CHUNK_S - t_s), (0, 0)))
            o_b, s_fin = _rwkv_scan([pad_t(a) for a in streams], lnw, lnb, state_wkv[e].astype(F32), SCAN_CHUNK_S,
                                    SCAN_SEQS_S)
            o_b = jnp.transpose(o_b[:, :t_s], (1, 0, 2)).reshape(m_s, D_B)
            xs = _ab_out_proj(olat, o_b, xs, wuv, wout, g_post, tm_s)
            ckv_s.append(jnp.transpose(ckv.reshape(t_s, n_bs, KV_LORA), (1, 0, 2)))
            kpe_s.append(jnp.transpose(kpe.reshape(t_s, n_bs, ROPE_DIM), (1, 0, 2)))
            wkv_s.append(s_fin)
            shift_s.append(zb[m_s - n_bs:])
        else:
            o = l // 2
            wp = w_pool[o].astype(BF16)
            ps = pool_scale[o][None]
            xp, tail = _pool_prompt(xp, g_pre, wp, ps, g_post, n_bp, t_p)
            pool_p.append(tail[:, 1:])
            prefix = jnp.transpose(state_pool[o].astype(F32), (1, 0, 2))
            xs, h_s = _pool_sample(xs, prefix.reshape((W_MAX - 1) * n_bs, D_MODEL), g_pre, wp, ps, g_post, n_bs, t_s)
            full = jnp.concatenate([prefix, h_s.reshape(t_s, n_bs, D_MODEL)], axis=0)
            pool_s.append(jnp.transpose(full[-(W_MAX - 1):], (1, 0, 2)))
        wu, wd = w_up[l].astype(BF16), w_down[l].astype(BF16)
        gfp, gfo = g_ffn_pre[l][None], g_ffn_post[l][None]
        xp = _ffn(xp, gfp, wu, wd, gfo, ROW_TILE)
        xs = _ffn(xs, gfp, wu, wd, gfo, tm_s)

    y_prompt = xp.reshape(n_bp, t_p, D_MODEL)[:, N_META:]
    y_sample = jnp.transpose(xs.reshape(t_s, n_bs, D_MODEL), (1, 0, 2))
    return (y_prompt, y_sample, jnp.stack(ckv_p), jnp.stack(kpe_p), jnp.stack(wkv_p), jnp.stack(shift_p),
            jnp.stack(pool_p), jnp.stack(ckv_s), jnp.stack(kpe_s), jnp.stack(wkv_s), jnp.stack(shift_s),
            jnp.stack(pool_s))
```

```python
import functools

import jax
import jax.numpy as jnp
from jax import lax
from jax.experimental import pallas as pl
from jax.experimental.pallas import tpu as pltpu

F32 = jnp.float32
BF16 = jnp.bfloat16

D_MODEL = 1024
N_META = 16
PAGE_SIZE = 128
H_A = 8
Q_LORA = 384
KV_LORA = 256
NOPE_DIM = 64
ROPE_DIM = 32
V_DIM = 64
ROPE_BASE = 10000.0
SM_SCALE = (NOPE_DIM + ROPE_DIM) ** -0.5
LOG2_E = 1.4426950408889634
H_B = 8
N_B = 64
D_B = H_B * N_B
DECAY_LORA = 64
AAA_LORA = 64
GATE_LORA = 160
RWKV_IN = 3 * D_B + DECAY_LORA + AAA_LORA + GATE_LORA
MLA_IN = Q_LORA + KV_LORA + ROPE_DIM
GN_EPS = 64e-5
POOL_WINDOWS = (2, 4, 8, 16)
POOL_GC = D_MODEL // len(POOL_WINDOWS)
W_MAX = 16
D_FF = 4 * D_MODEL
RMS_EPS = 1e-6

LANES = 128
QK_PAD = KV_LORA + LANES
VMEM_LIMIT = 56 * 1024 * 1024
ROW_TILE = 384
ATT_Q_TILE = 688
POOL_TILE = 688
PREP_TILE = 344
SCAN_CHUNK_P = 48
SCAN_CHUNK_S = 8
SCAN_SEQS_P = 2
SCAN_SEQS_S = 4
NEW_KEY_ROWS = 16
PAGES_PER_CHUNK = 8
FFN_STAGE_BYTES = 2 * 1024 * 1024


def _cparams(sem):
    return pltpu.CompilerParams(dimension_semantics=sem, vmem_limit_bytes=VMEM_LIMIT)


def _rms(x, g):
    return x * lax.rsqrt(jnp.mean(x * x, axis=-1, keepdims=True) + RMS_EPS) * g


def _dot(a, b):
    return jnp.dot(a, b, preferred_element_type=F32)


def _dot_nt(a, b):
    return lax.dot_general(a, b, (((1,), (1,)), ((), ())), preferred_element_type=F32)


def _dot_tn(a, b):
    return lax.dot_general(a, b, (((0,), (0,)), ((), ())), preferred_element_type=F32)


def _full(shape):
    n = len(shape)
    return pl.BlockSpec(shape, lambda *_: (0,) * n)


def _ab_in_kernel(x_ref, gpre_ref, wq2_ref, wkv_ref, wb_ref, gq_ref, gkv_ref, wuq_ref, wuk_ref, cos_ref, sin_ref,
                  ckv_ref, kpe_ref, zb_ref, qabs_ref, kvb_ref):
    h = _rms(x_ref[...], gpre_ref[...]).astype(BF16)
    zq2 = _dot(h, wq2_ref[...])
    zkv = _dot(h, wkv_ref[...])
    zb_ref[...] = _dot(h, wb_ref[...])
    ckv = _rms(zkv, gkv_ref[...])
    ckv_ref[...] = ckv
    cos = cos_ref[...]
    sin = sin_ref[...]
    pe = zq2[:, Q_LORA:Q_LORA + ROPE_DIM]
    pe_sw = zq2[:, Q_LORA + ROPE_DIM:Q_LORA + 2 * ROPE_DIM]
    kpe = pe * cos[:, :ROPE_DIM] + pe_sw * sin[:, :ROPE_DIM]
    kpe_ref[...] = kpe
    kvb_ref[:, :KV_LORA] = ckv.astype(BF16)
    kvb_ref[:, KV_LORA:] = jnp.concatenate([kpe, jnp.zeros((kpe.shape[0], LANES - ROPE_DIM), F32)], axis=1).astype(BF16)
    qn = _rms(zq2[:, :Q_LORA], gq_ref[...]).astype(BF16)
    q2 = _dot(qn, wuq_ref[...])
    n_nope = H_A * NOPE_DIM
    for hd in range(H_A):
        qn_h = q2[:, hd * NOPE_DIM:(hd + 1) * NOPE_DIM].astype(BF16)
        qabs_ref[hd, :, 0:KV_LORA] = _dot(qn_h, wuk_ref[hd]).astype(BF16)
        r0 = n_nope + hd * LANES
        r1 = n_nope + (H_A + hd) * LANES
        qabs_ref[hd, :, KV_LORA:QK_PAD] = (q2[:, r0:r0 + LANES] * cos + q2[:, r1:r1 + LANES] * sin).astype(BF16)


def _ab_in_proj(x, g_pre, wts, cos_t, sin_t, tm):
    m = x.shape[0]
    wq2, wkv, wb, g_q, g_kv, wuq, wuk = wts
    row = lambda w: pl.BlockSpec((tm, w), lambda i: (i, 0))
    return pl.pallas_call(
        _ab_in_kernel,
        grid=(m // tm,),
        in_specs=[row(D_MODEL), _full(g_pre.shape), _full(wq2.shape), _full(wkv.shape), _full(wb.shape),
                  _full(g_q.shape), _full(g_kv.shape), _full(wuq.shape), _full(wuk.shape), row(LANES), row(LANES)],
        out_specs=[row(KV_LORA), row(ROPE_DIM), row(RWKV_IN),
                   pl.BlockSpec((H_A, tm, QK_PAD), lambda i: (0, i, 0)), row(QK_PAD)],
        out_shape=[jax.ShapeDtypeStruct((m, KV_LORA), F32), jax.ShapeDtypeStruct((m, ROPE_DIM), F32),
                   jax.ShapeDtypeStruct((m, RWKV_IN), F32), jax.ShapeDtypeStruct((H_A, m, QK_PAD), BF16),
                   jax.ShapeDtypeStruct((m, QK_PAD), BF16)],
        compiler_params=_cparams(("parallel",)),
        name="ab_in_proj",
    )(x, g_pre, wq2, wkv, wb, g_q, g_kv, wuq, wuk, cos_t, sin_t)


def _attn_prompt_kernel(q_ref, kv_ref, o_ref, kpad, *, t_p, tq):
    t_pad = kpad.shape[0]

    @pl.when(pl.program_id(1) == 0)
    def _():
        kpad[:t_p, :] = kv_ref[0]
        kpad[t_p:, :] = jnp.zeros((t_pad - t_p, QK_PAD), BF16)

    for i in range(t_p // tq):
        q0 = i * tq
        kext = min(-(-(q0 + tq) // LANES) * LANES, t_pad)
        lo = (q0 // LANES) * LANES
        k = kpad[:kext, :]
        s = _dot_nt(q_ref[0, q0:q0 + tq, :], k)
        q_pos = q0 + lax.broadcasted_iota(jnp.int32, (tq, kext - lo), 0)
        k_pos = lo + lax.broadcasted_iota(jnp.int32, (tq, kext - lo), 1)
        s_diag = jnp.where(k_pos <= q_pos, s[:, lo:], -jnp.inf)
        s = jnp.concatenate([s[:, :lo], s_diag], axis=1) if lo else s_diag
        p = jnp.exp2((s - jnp.max(s, axis=-1, keepdims=True)) * (SM_SCALE * LOG2_E))
        l = jnp.sum(p, axis=-1, keepdims=True)
        o = _dot(p.astype(BF16), k[:, :KV_LORA])
        o_ref[0, q0:q0 + tq, :] = (o / l).astype(BF16)


def _attn_prompt(qabs, kvb, n_b, t_p):
    m = qabs.shape[1]
    t_pad = -(-t_p // LANES) * LANES
    return pl.pallas_call(
        functools.partial(_attn_prompt_kernel, t_p=t_p, tq=ATT_Q_TILE),
        grid=(n_b, H_A),
        in_specs=[pl.BlockSpec((1, t_p, QK_PAD), lambda b, h: (h, b, 0)),
                  pl.BlockSpec((1, t_p, QK_PAD), lambda b, h: (b, 0, 0))],
        out_specs=pl.BlockSpec((1, t_p, KV_LORA), lambda b, h: (h, b, 0)),
        out_shape=jax.ShapeDtypeStruct((H_A, m, KV_LORA), BF16),
        scratch_shapes=[pltpu.VMEM((t_pad, QK_PAD), BF16)],
        compiler_params=_cparams(("parallel", "arbitrary")),
        name="attn_prompt",
    )(qabs, kvb.reshape(n_b, t_p, QK_PAD))


def _attn_sample_kernel(pt_ref, q_ref, knew_ref, ckv_hbm, kpe_hbm, o_ref, ckv_buf, kpe_buf, s_scr, sems, *, layer,
                        n_pages, t_s):
    b = pl.program_id(0)
    n_b = pl.num_programs(0)
    slot = b % 2

    def page_copies(bb, sl, p):
        page = pt_ref[bb, p]
        return (pltpu.make_async_copy(ckv_hbm.at[layer, page], ckv_buf.at[sl, p], sems.at[0, sl]),
                pltpu.make_async_copy(kpe_hbm.at[layer, page], kpe_buf.at[sl, p], sems.at[1, sl]))

    def start_all(bb, sl):
        for p in range(n_pages):
            for cp in page_copies(bb, sl, p):
                cp.start()

    @pl.when(b == 0)
    def _():
        start_all(0, 0)

    @pl.when(b + 1 < n_b)
    def _():
        start_all(b + 1, 1 - slot)

    for p in range(n_pages):
        for cp in page_copies(b, slot, p):
            cp.wait()

    q = q_ref[0]
    q_lat = q[:, :KV_LORA]
    q_pe = q[:, KV_LORA:KV_LORA + ROPE_DIM]
    rows = PAGES_PER_CHUNK * PAGE_SIZE
    n_chunks = n_pages // PAGES_PER_CHUNK

    def latent_rows(c):
        return ckv_buf[slot, c * PAGES_PER_CHUNK:(c + 1) * PAGES_PER_CHUNK].reshape(rows, KV_LORA).astype(BF16)

    for c in range(n_chunks):
        kp = jnp.concatenate([kpe_buf[slot, c * PAGES_PER_CHUNK + i] for i in range(PAGES_PER_CHUNK)], axis=1)
        s_scr[:, c * rows:(c + 1) * rows] = (_dot_nt(q_lat, latent_rows(c)) + _dot(q_pe, kp.astype(BF16))) * SM_SCALE
    kn = knew_ref[0]
    s_new = _dot_nt(q, kn) * SM_SCALE
    step = lax.broadcasted_iota(jnp.int32, s_new.shape, 0) % t_s
    key = lax.broadcasted_iota(jnp.int32, s_new.shape, 1)
    s_new = jnp.where(key <= step, s_new, -jnp.inf)
    mx = s_scr[:, 0:rows]
    for c in range(1, n_chunks):
        mx = jnp.maximum(mx, s_scr[:, c * rows:(c + 1) * rows])
    m = jnp.maximum(jnp.max(mx, axis=-1, keepdims=True), jnp.max(s_new, axis=-1, keepdims=True))
    p_new = jnp.exp(s_new - m)
    acc = _dot(p_new.astype(BF16), kn[:, :KV_LORA])
    p_sum = jnp.zeros((q.shape[0], rows), F32)
    for c in range(n_chunks):
        p = jnp.exp(s_scr[:, c * rows:(c + 1) * rows] - m)
        p_sum = p_sum + p
        acc = acc + _dot(p.astype(BF16), latent_rows(c))
    l = jnp.sum(p_sum, axis=-1, keepdims=True) + jnp.sum(p_new, axis=-1, keepdims=True)
    o_ref[0] = (acc / l).astype(BF16)


def _attn_sample(page_table, q_s, k_new, cache_ckv, cache_kpe, layer, t_s):
    n_b, n_pages = page_table.shape
    n_q = q_s.shape[1]
    grid_spec = pltpu.PrefetchScalarGridSpec(
        num_scalar_prefetch=1,
        grid=(n_b,),
        in_specs=[pl.BlockSpec((1, n_q, QK_PAD), lambda b, pt: (b, 0, 0)),
                  pl.BlockSpec((1,) + k_new.shape[1:], lambda b, pt: (b, 0, 0)),
                  pl.BlockSpec(memory_space=pl.ANY),
                  pl.BlockSpec(memory_space=pl.ANY)],
        out_specs=pl.BlockSpec((1, n_q, KV_LORA), lambda b, pt: (b, 0, 0)),
        scratch_shapes=[pltpu.VMEM((2, n_pages, PAGE_SIZE, KV_LORA), F32),
                        pltpu.VMEM((2, n_pages, ROPE_DIM, PAGE_SIZE), F32),
                        pltpu.VMEM((n_q, n_pages * PAGE_SIZE), F32),
                        pltpu.SemaphoreType.DMA((2, 2))],
    )
    return pl.pallas_call(
        functools.partial(_attn_sample_kernel, layer=layer, n_pages=n_pages, t_s=t_s),
        grid_spec=grid_spec,
        out_shape=jax.ShapeDtypeStruct((n_b, n_q, KV_LORA), BF16),
        compiler_params=_cparams(("arbitrary",)),
        name="attn_sample",
    )(page_table, q_s, k_new, cache_ckv, cache_kpe)


def _ab_out_kernel(olat_ref, ob_ref, x_ref, wuv_ref, wout_ref, gpost_ref, o_ref, cat_ref):
    n_a = H_A * V_DIM
    for hd in range(0, H_A, 2):
        pair = [_dot(olat_ref[hd + i], wuv_ref[hd + i]) for i in range(2)]
        cat_ref[:, hd * V_DIM:(hd + 2) * V_DIM] = jnp.concatenate(pair, axis=1).astype(BF16)
    cat_ref[:, n_a:] = ob_ref[...].astype(BF16)
    y = _dot(cat_ref[...], wout_ref[...])
    o_ref[...] = x_ref[...] + _rms(y, gpost_ref[...])


def _ab_out_proj(olat, o_b, x, wuv, wout, g_post, tm):
    m = x.shape[0]
    row = lambda w: pl.BlockSpec((tm, w), lambda i: (i, 0))
    return pl.pallas_call(
        _ab_out_kernel,
        grid=(m // tm,),
        in_specs=[pl.BlockSpec((H_A, tm, KV_LORA), lambda i: (0, i, 0)), row(D_B), row(D_MODEL),
                  _full(wuv.shape), _full(wout.shape), _full(g_post.shape)],
        out_specs=row(D_MODEL),
        out_shape=jax.ShapeDtypeStruct((m, D_MODEL), F32),
        scratch_shapes=[pltpu.VMEM((tm, H_A * V_DIM + D_B), BF16)],
        compiler_params=_cparams(("parallel",)),
        name="ab_out_proj",
    )(olat, o_b, x, wuv, wout, g_post)


def _ffn_kernel(x_ref, gpre_ref, wup_hbm, wdown_hbm, gpost_ref, o_ref, wup_ref, wdown_ref, st_up, st_dn, sems, *,
                ff_chunk, layer):
    @pl.when(pl.program_id(0) == 0)
    def _():
        jobs = []
        for k, (src, dst, st) in enumerate(((wup_hbm, wup_ref, st_up), (wdown_hbm, wdown_ref, st_dn))):
            rows = st.shape[1]
            for c in range(dst.shape[0] // rows):
                jobs.append((k, c, rows, src, dst, st))

        def copy(job, slot):
            k, c, rows, src, _, st = job
            return pltpu.make_async_copy(src.at[layer, c * rows:(c + 1) * rows], st.at[slot], sems.at[k, slot])

        copy(jobs[0], 0).start()
        for i, job in enumerate(jobs):
            slot = i % 2
            if i + 1 < len(jobs):
                copy(jobs[i + 1], 1 - slot).start()
            copy(job, slot).wait()
            k, c, rows, _, dst, st = job
            dst[c * rows:(c + 1) * rows, :] = st[slot].astype(BF16)

    x = x_ref[...]
    h = _rms(x, gpre_ref[...]).astype(BF16)
    y = jnp.zeros(x.shape, F32)
    for c in range(D_FF // ff_chunk):
        u = _dot(h, wup_ref[:, c * ff_chunk:(c + 1) * ff_chunk])
        a = jnp.square(jnp.maximum(u, 0.0)).astype(BF16)
        y = y + _dot(a, wdown_ref[c * ff_chunk:(c + 1) * ff_chunk, :])
    o_ref[...] = x + _rms(y, gpost_ref[...])


def _ffn(x, g_pre, w_up, w_down, g_post, layer, tm):
    m = x.shape[0]
    row = pl.BlockSpec((tm, D_MODEL), lambda i: (i, 0))
    return pl.pallas_call(
        functools.partial(_ffn_kernel, ff_chunk=1024, layer=layer),
        grid=(m // tm,),
        in_specs=[row, _full(g_pre.shape), pl.BlockSpec(memory_space=pl.ANY), pl.BlockSpec(memory_space=pl.ANY),
                  _full(g_post.shape)],
        out_specs=row,
        out_shape=jax.ShapeDtypeStruct((m, D_MODEL), F32),
        scratch_shapes=[pltpu.VMEM(w_up.shape[1:], BF16), pltpu.VMEM(w_down.shape[1:], BF16),
                        pltpu.VMEM((2, FFN_STAGE_BYTES // (4 * D_FF), D_FF), F32),
                        pltpu.VMEM((2, FFN_STAGE_BYTES // (4 * D_MODEL), D_MODEL), F32),
                        pltpu.SemaphoreType.DMA((2, 2))],
        compiler_params=_cparams(("arbitrary",)),
        name="ffn",
    )(x, g_pre, w_up, w_down, g_post)


def _split_dot(x, w):
    hi = x.astype(BF16)
    lo = (x - hi.astype(F32)).astype(BF16)
    return _dot(hi, w) + _dot(lo, w)


def _rwkv_prep_kernel(zb_ref, first_ref, mu_ref, w0_ref, w2_ref, a0_ref, a2_ref, g2_ref, kk_ref, ka_ref, rk_ref,
                      ones_ref, r_o, lw_o, keff_o, v_o, kn_o, b_o, g_o, bonus_o, carry, *, shift):
    @pl.when(pl.program_id(1) == 0)
    def _():
        carry[...] = first_ref[0]

    zb = zb_ref[...]
    tm = zb.shape[0]
    if shift == 1:
        row = lax.broadcasted_iota(jnp.int32, (tm, 1), 0)
        prev = jnp.where(row == 0, carry[0:1, :], pltpu.roll(zb, 1, 0))
        carry[0:1, :] = zb[tm - 1:tm, :]
    else:
        prev = jnp.concatenate([carry[...], zb[:tm - shift]], axis=0)
        carry[...] = zb[tm - shift:]
    xm = zb + (prev - zb) * mu_ref[...]
    o3 = 3 * D_B
    o4 = o3 + DECAY_LORA
    o5 = o4 + AAA_LORA
    r = xm[:, :D_B]
    k = xm[:, D_B:2 * D_B]
    v = xm[:, 2 * D_B:o3]
    z = -(w0_ref[...] + _dot(jnp.tanh(xm[:, o3:o4]).astype(BF16), w2_ref[...].astype(BF16)))
    softplus = jnp.maximum(z, 0.0) + jnp.log(1.0 + jnp.exp(-jnp.abs(z)))
    lw_o[...] = -jnp.exp(-softplus - 0.5)
    a = 1.0 / (1.0 + jnp.exp(-(a0_ref[...] + _dot(xm[:, o4:o5].astype(BF16), a2_ref[...].astype(BF16)))))
    sg = 1.0 / (1.0 + jnp.exp(-xm[:, o5:]))
    g_o[...] = _dot(sg.astype(BF16), g2_ref[...].astype(BF16))
    ones = ones_ref[...]
    kk = k * kk_ref[...]
    kk = kk / jnp.maximum(jnp.sqrt(_split_dot(kk * kk, ones)), 1e-12)
    keff = k * (1.0 + (a - 1.0) * ka_ref[...])
    r_o[...] = r
    keff_o[...] = keff
    v_o[...] = v
    kn_o[...] = kk
    b_o[...] = kk * a
    bonus_o[...] = _split_dot(r * keff * rk_ref[...], ones) * v


def _rwkv_prep(zb, first, rw, tm, shift):
    m = zb.shape[0]
    n_seq = first.shape[0]
    n_t = m // (n_seq * tm)
    row = lambda w: pl.BlockSpec((tm, w), lambda b, j: (b * n_t + j, 0))
    return pl.pallas_call(
        functools.partial(_rwkv_prep_kernel, shift=shift),
        grid=(n_seq, n_t),
        in_specs=[row(RWKV_IN), pl.BlockSpec((1,) + first.shape[1:], lambda b, j: (b, 0, 0))]
        + [_full(w.shape) for w in rw],
        out_specs=[row(D_B)] * 8,
        out_shape=[jax.ShapeDtypeStruct((m, D_B), F32)] * 8,
        scratch_shapes=[pltpu.VMEM(first.shape[1:], F32)],
        compiler_params=_cparams(("parallel", "arbitrary")),
        name="rwkv_prep",
    )(zb, first, *rw)


def _rwkv_scan_kernel(r_ref, lw_ref, keff_ref, v_ref, kn_ref, b_ref, g_ref, bonus_ref, lnw_ref, lnb_ref, s0_ref,
                      tri_ref, o_ref, s_out_ref, s_scr, *, chunk, n_sb):
    c = pl.program_id(1)

    @pl.when(c == 0)
    def _():
        s_scr[...] = s0_ref[...]

    t_i = lax.broadcasted_iota(jnp.int32, (chunk, chunk), 0)
    s_i = lax.broadcasted_iota(jnp.int32, (chunk, chunk), 1)
    strict = s_i < t_i
    incl = s_i <= t_i
    eye = jnp.where(s_i == t_i, 1.0, 0.0).astype(F32)
    levels = []
    k_lvl = 1
    while (1 << (k_lvl - 1)) < chunk:
        same_new = (t_i >> k_lvl) == (s_i >> k_lvl)
        same_old = (t_i >> (k_lvl - 1)) == (s_i >> (k_lvl - 1))
        levels.append(strict & same_new & jnp.logical_not(same_old))
        k_lvl += 1

    tri = tri_ref[...]
    chains = [(j, hd) for j in range(n_sb) for hd in range(H_B)]
    qr, bk, kk, bk_e, kk_e, v, kq32, rq32, g_last = {}, {}, {}, {}, {}, {}, {}, {}, {}
    for j in range(n_sb):
        lw = lw_ref[j]
        hi = lw.astype(BF16)
        r1 = lw - hi.astype(F32)
        mid = r1.astype(BF16)
        lo = (r1 - mid.astype(F32)).astype(BF16)
        cs = _dot(tri, hi) + _dot(tri, mid) + _dot(tri, lo)
        cs_last = cs[chunk - 1:chunk, :]
        g_inv = jnp.exp(-cs)
        g_end = jnp.exp(cs_last - cs)
        kq_j = kn_ref[j] * jnp.exp(cs - lw)
        rq_j = r_ref[j] * jnp.exp(cs)
        qr_j = jnp.concatenate([kq_j, rq_j], axis=0).astype(BF16)
        b_j = b_ref[j]
        ke_j = keff_ref[j]
        bk_j = (b_j * g_inv).astype(BF16)
        kk_j = (ke_j * g_inv).astype(BF16)
        bke_j = (b_j * g_end).astype(BF16)
        kke_j = (ke_j * g_end).astype(BF16)
        v_j = v_ref[j].astype(BF16)
        gl_j = jnp.exp(cs_last)
        for hd in range(H_B):
            sl = slice(hd * N_B, (hd + 1) * N_B)
            ch = (j, hd)
            qr[ch], bk[ch], kk[ch], bk_e[ch], kk_e[ch], v[ch] = (qr_j[:, sl], bk_j[:, sl], kk_j[:, sl], bke_j[:, sl],
                                                                 kke_j[:, sl], v_j[:, sl])
            kq32[ch], rq32[ch], g_last[ch] = kq_j[:, sl], rq_j[:, sl], gl_j[:, sl]

    p1 = {ch: _dot_nt(qr[ch], bk[ch]) for ch in chains}
    p2 = {ch: _dot_nt(qr[ch], kk[ch]) for ch in chains}
    a_m = {ch: -p1[ch][:chunk] for ch in chains}
    t_m = {ch: eye + jnp.where(levels[0], a_m[ch], 0.0) for ch in chains}
    w_m = {ch: _dot(jnp.where(strict, p2[ch][:chunk], 0.0).astype(BF16), v[ch]) for ch in chains}
    for lvl in levels[1:]:
        x_m = {ch: _dot(jnp.where(lvl, a_m[ch], 0.0).astype(BF16), t_m[ch].astype(BF16)) for ch in chains}
        t_m = {ch: t_m[ch] + _dot(t_m[ch].astype(BF16), x_m[ch].astype(BF16)) for ch in chains}
    z_m = {ch: _dot(t_m[ch].astype(BF16), jnp.concatenate([kq32[ch], w_m[ch]], axis=1).astype(BF16))
           for ch in chains}
    z_b = {ch: z_m[ch].astype(BF16) for ch in chains}
    d1 = {ch: _dot(jnp.where(incl, -p1[ch][chunk:], 0.0).astype(BF16), z_b[ch]) for ch in chains}
    d2 = {ch: _dot(jnp.where(incl, p2[ch][chunk:], 0.0).astype(BF16), v[ch]) for ch in chains}
    mn = {ch: _dot_tn(z_b[ch], bk_e[ch]) for ch in chains}
    vk = {ch: _dot_tn(v[ch], kk_e[ch]) for ch in chains}

    s_old = {ch: s_scr[ch[0], ch[1]] for ch in chains}
    s_b = {ch: s_old[ch].astype(BF16) for ch in chains}
    y = {ch: _dot_nt((rq32[ch] + d1[ch][:, :N_B]).astype(BF16), s_b[ch]) + d1[ch][:, N_B:] + d2[ch] for ch in chains}
    s_new = {ch: s_old[ch] * g_last[ch] - _dot(s_b[ch], mn[ch][:N_B].astype(BF16)) + (vk[ch] - mn[ch][N_B:])
             for ch in chains}
    for ch in chains:
        s_scr[ch[0], ch[1]] = s_new[ch]
        s_out_ref[ch[0], ch[1]] = s_new[ch]
    for j in range(n_sb):
        outs = []
        for hd in range(H_B):
            y_h = y[(j, hd)]
            mean = jnp.mean(y_h, axis=-1, keepdims=True)
            var = jnp.mean(jnp.square(y_h - mean), axis=-1, keepdims=True)
            outs.append((y_h - mean) * lax.rsqrt(var + GN_EPS))
        yn = jnp.concatenate(outs, axis=-1) * lnw_ref[...] + lnb_ref[...]
        o_ref[j] = (yn + bonus_ref[j]) * g_ref[j]


def _rwkv_scan(streams, ln_w, ln_b, s0, chunk, n_sb):
    n_seq, t, _ = streams[0].shape
    n_chunks = t // chunk
    tri = jnp.tril(jnp.ones((chunk, chunk), F32)).astype(BF16)
    row = pl.BlockSpec((n_sb, chunk, D_B), lambda b, c: (b, c, 0))
    st = pl.BlockSpec((n_sb, H_B, N_B, N_B), lambda b, c: (b, 0, 0, 0))
    return pl.pallas_call(
        functools.partial(_rwkv_scan_kernel, chunk=chunk, n_sb=n_sb),
        grid=(n_seq // n_sb, n_chunks),
        in_specs=[row] * 8 + [_full(ln_w.shape), _full(ln_b.shape), st, _full(tri.shape)],
        out_specs=[row, st],
        out_shape=[jax.ShapeDtypeStruct((n_seq, t, D_B), F32), jax.ShapeDtypeStruct(s0.shape, F32)],
        scratch_shapes=[pltpu.VMEM((n_sb, H_B, N_B, N_B), F32)],
        compiler_params=_cparams(("parallel", "arbitrary")),
        name="rwkv_scan",
    )(*streams, ln_w, ln_b, s0, tri)


def _pool_prompt_kernel(x_ref, gpre_ref, wpool_ref, pscale_ref, gpost_ref, o_ref, tail_ref, hext, *, tm):
    j = pl.program_id(1)
    x = x_ref[...]
    h = _rms(x, gpre_ref[...])

    @pl.when(j == 0)
    def _():
        hext[0:W_MAX, :] = jnp.zeros((W_MAX, D_MODEL), F32)

    @pl.when(j > 0)
    def _():
        hext[0:W_MAX, :] = hext[tm:tm + W_MAX, :]

    hext[W_MAX:, :] = h
    tail_ref[0] = h[tm - W_MAX:, :]
    pos = j * tm + lax.broadcasted_iota(jnp.int32, (tm, 1), 0)
    acc = hext[...]
    ys = []
    for gi, w in enumerate(POOL_WINDOWS):
        acc = acc[:, POOL_GC * (1 if gi else 0):]
        acc = acc + pltpu.roll(acc, w // 2, 0)
        cnt = jnp.minimum(pos + 1, w).astype(F32)
        pooled = acc[W_MAX:, :POOL_GC] / cnt - h[:, gi * POOL_GC:(gi + 1) * POOL_GC]
        ys.append(_dot(pooled.astype(BF16), wpool_ref[gi]))
    y = jnp.concatenate(ys, axis=-1) * pscale_ref[...]
    o_ref[...] = x + _rms(y, gpost_ref[...])


def _pool_prompt(x, g_pre, w_pool, p_scale, g_post, n_b, t_p):
    m = x.shape[0]
    tm = POOL_TILE
    n_t = t_p // tm
    row = pl.BlockSpec((tm, D_MODEL), lambda b, j: (b * n_t + j, 0))
    return pl.pallas_call(
        functools.partial(_pool_prompt_kernel, tm=tm),
        grid=(n_b, n_t),
        in_specs=[row, _full(g_pre.shape), _full(w_pool.shape), _full(p_scale.shape), _full(g_post.shape)],
        out_specs=[row, pl.BlockSpec((1, W_MAX, D_MODEL), lambda b, j: (b, 0, 0))],
        out_shape=[jax.ShapeDtypeStruct((m, D_MODEL), F32), jax.ShapeDtypeStruct((n_b, W_MAX, D_MODEL), F32)],
        scratch_shapes=[pltpu.VMEM((W_MAX + tm, D_MODEL), F32)],
        compiler_params=_cparams(("parallel", "arbitrary")),
        name="pool_prompt",
    )(x, g_pre, w_pool, p_scale, g_post)


def _pool_sample_kernel(x_ref, pre_ref, gpre_ref, wpool_ref, pscale_ref, gpost_ref, o_ref, h_ref, *, n_b, t_s):
    x = x_ref[...]
    h = _rms(x, gpre_ref[...])
    h_ref[...] = h
    n_pre = W_MAX - 1
    rows = [pre_ref[i * n_b:(i + 1) * n_b, :] for i in range(n_pre)] + [h[t * n_b:(t + 1) * n_b, :] for t in range(t_s)]
    outs = []
    for t in range(t_s):
        ys = []
        for gi, w in enumerate(POOL_WINDOWS):
            sl = slice(gi * POOL_GC, (gi + 1) * POOL_GC)
            win = rows[n_pre + t][:, sl]
            for d in range(1, w):
                win = win + rows[n_pre + t - d][:, sl]
            pooled = win / float(w) - rows[n_pre + t][:, sl]
            ys.append(_dot(pooled.astype(BF16), wpool_ref[gi]))
        outs.append(jnp.concatenate(ys, axis=-1))
    y = jnp.concatenate(outs, axis=0) * pscale_ref[...]
    o_ref[...] = x + _rms(y, gpost_ref[...])


def _pool_sample(x, prefix, g_pre, w_pool, p_scale, g_post, n_b, t_s):
    args = (x, prefix, g_pre, w_pool, p_scale, g_post)
    return pl.pallas_call(
        functools.partial(_pool_sample_kernel, n_b=n_b, t_s=t_s),
        grid=(1,),
        in_specs=[_full(a.shape) for a in args],
        out_specs=[_full(x.shape), _full(x.shape)],
        out_shape=[jax.ShapeDtypeStruct(x.shape, F32)] * 2,
        compiler_params=_cparams(("arbitrary",)),
        name="pool_sample",
    )(*args)


def _rope_tables(pos):
    half = ROPE_DIM // 2
    inv = ROPE_BASE ** (-jnp.arange(half, dtype=F32) / half)
    ang = pos[:, None] * inv[None, :]
    cos, sin = jnp.cos(ang), jnp.sin(ang)
    reps = LANES // ROPE_DIM
    return jnp.tile(jnp.concatenate([cos, cos], -1), (1, reps)), jnp.tile(jnp.concatenate([-sin, sin], -1), (1, reps))


def _swap_halves(w):
    half = ROPE_DIM // 2
    return jnp.concatenate([w[..., half:], w[..., :half]], axis=-1)


def _ab_weights(e, w_in, g_q, w_uq, g_kv, w_uk):
    w = w_in[e]
    w_pe = w[:, Q_LORA + KV_LORA:MLA_IN]
    wq2 = jnp.concatenate([w[:, :Q_LORA], w_pe, _swap_halves(w_pe)], axis=1).astype(BF16)
    wkv = w[:, Q_LORA:Q_LORA + KV_LORA].astype(BF16)
    wb = w[:, MLA_IN:].astype(BF16)
    uq = w_uq[e].reshape(Q_LORA, H_A, NOPE_DIM + ROPE_DIM)
    pad = ((0, 0), (0, 0), (0, LANES - ROPE_DIM))
    uq_pe = uq[:, :, NOPE_DIM:]
    wuq = jnp.concatenate([uq[:, :, :NOPE_DIM].reshape(Q_LORA, H_A * NOPE_DIM),
                           jnp.pad(uq_pe, pad).reshape(Q_LORA, H_A * LANES),
                           jnp.pad(_swap_halves(uq_pe), pad).reshape(Q_LORA, H_A * LANES)], axis=1).astype(BF16)
    wuk = jnp.transpose(w_uk[e], (1, 2, 0)).astype(BF16)
    return wq2, wkv, wb, g_q[e][None], g_kv[e][None], wuq, wuk


def _rwkv_weights(e, mu_shift, w0, w2, a0, a2, g2, k_k, k_a, r_k):
    head = jnp.arange(D_B) // N_B
    ones = (head[:, None] == head[None, :]).astype(BF16)
    return (mu_shift[e][None], w0[e][None], w2[e], a0[e][None], a2[e], g2[e],
            k_k[e][None], k_a[e][None], r_k[e].reshape(1, D_B), ones)


def kernel(x_prompt, x_sample, cache_ckv, cache_kpe, page_table, state_wkv, state_shift, state_pool, meta_tokens,
           g_mix_pre, g_mix_post, g_ffn_pre, g_ffn_post, w_in, g_q, w_uq, g_kv, w_uk, w_uv, mu_shift, w0, w2, a0, a2,
           g2, k_k, k_a, r_k, ln_w, ln_b, w_out, w_pool, pool_scale, w_up, w_down):
    n_bp, seq, _ = x_prompt.shape
    n_bs, t_s, _ = x_sample.shape
    depth = g_mix_pre.shape[0]
    t_p = seq + N_META
    n_pages = page_table.shape[1]
    past = n_pages * PAGE_SIZE
    m_p = n_bp * t_p
    m_s = n_bs * t_s

    meta = jnp.broadcast_to(meta_tokens[None].astype(x_prompt.dtype), (n_bp, N_META, D_MODEL))
    xp = jnp.concatenate([meta, x_prompt], axis=1).reshape(m_p, D_MODEL)
    xs = jnp.transpose(x_sample, (1, 0, 2)).reshape(m_s, D_MODEL)

    cos_p, sin_p = _rope_tables(jnp.tile(jnp.arange(t_p, dtype=F32), n_bp))
    cos_s, sin_s = _rope_tables(jnp.repeat(past + jnp.arange(t_s, dtype=F32), n_bs))
    tm_s = m_s
    cache_kpe_t = jnp.swapaxes(cache_kpe, 2, 3)

    ckv_p, kpe_p, wkv_p, shift_p, pool_p = [], [], [], [], []
    ckv_s, kpe_s, wkv_s, shift_s, pool_s = [], [], [], [], []
    for l in range(depth):
        g_pre, g_post = g_mix_pre[l][None], g_mix_post[l][None]
        if l % 2 == 0:
            e = l // 2
            ab_w = _ab_weights(e, w_in, g_q, w_uq, g_kv, w_uk)
            rw = _rwkv_weights(e, mu_shift, w0, w2, a0, a2, g2, k_k, k_a, r_k)
            wuv = jnp.transpose(w_uv[e], (1, 0, 2)).astype(BF16)
            wout = w_out[e].astype(BF16)
            lnw, lnb = ln_w[e][None], ln_b[e][None]

            ckv, kpe, zb, qabs, kvb = _ab_in_proj(xp, g_pre, ab_w, cos_p, sin_p, ROW_TILE)
            olat = _attn_prompt(qabs, kvb, n_bp, t_p)
            zb3 = zb.reshape(n_bp, t_p, RWKV_IN)
            first = jnp.zeros((n_bp, 8, RWKV_IN), F32)
            streams = [a.reshape(n_bp, t_p, D_B) for a in _rwkv_prep(zb, first, rw, PREP_TILE, 1)]
            o_b, s_fin = _rwkv_scan(streams, lnw, lnb, jnp.zeros((n_bp, H_B, N_B, N_B), F32), SCAN_CHUNK_P,
                                    SCAN_SEQS_P)
            xp = _ab_out_proj(olat, o_b.reshape(m_p, D_B), xp, wuv, wout, g_post, ROW_TILE)
            ckv_p.append(ckv.reshape(n_bp, t_p, KV_LORA))
            kpe_p.append(kpe.reshape(n_bp, t_p, ROPE_DIM))
            wkv_p.append(s_fin)
            shift_p.append(zb3[:, -1])

            ckv, kpe, zb, qabs, kvb = _ab_in_proj(xs, g_pre, ab_w, cos_s, sin_s, tm_s)
            q_s = jnp.transpose(qabs.reshape(H_A, t_s, n_bs, QK_PAD), (2, 0, 1, 3)).reshape(n_bs, H_A * t_s, QK_PAD)
            k_new = jnp.pad(jnp.transpose(kvb.reshape(t_s, n_bs, QK_PAD), (1, 0, 2)),
                            ((0, 0), (0, NEW_KEY_ROWS - t_s), (0, 0)))
            o_s = _attn_sample(page_table, q_s, k_new, cache_ckv, cache_kpe_t, e, t_s)
            olat = jnp.transpose(o_s.reshape(n_bs, H_A, t_s, KV_LORA), (1, 2, 0, 3)).reshape(H_A, m_s, KV_LORA)
            streams = _rwkv_prep(zb, state_shift[e].astype(F32)[None], rw, tm_s, n_bs)
            pad_t = lambda a: jnp.pad(jnp.transpose(a.reshape(t_s, n_bs, D_B), (1, 0, 2)),
                                      ((0, 0), (0, SCAN_CHUNK_S - t_s), (0, 0)))
            o_b, s_fin = _rwkv_scan([pad_t(a) for a in streams], lnw, lnb, state_wkv[e].astype(F32), SCAN_CHUNK_S,
                                    SCAN_SEQS_S)
            o_b = jnp.transpose(o_b[:, :t_s], (1, 0, 2)).reshape(m_s, D_B)
            xs = _ab_out_proj(olat, o_b, xs, wuv, wout, g_post, tm_s)
            ckv_s.append(jnp.transpose(ckv.reshape(t_s, n_bs, KV_LORA), (1, 0, 2)))
            kpe_s.append(jnp.transpose(kpe.reshape(t_s, n_bs, ROPE_DIM), (1, 0, 2)))
            wkv_s.append(s_fin)
            shift_s.append(zb[m_s - n_bs:])
        else:
            o = l // 2
            wp = w_pool[o].astype(BF16)
            ps = pool_scale[o][None]
            xp, tail = _pool_prompt(xp, g_pre, wp, ps, g_post, n_bp, t_p)
            pool_p.append(tail[:, 1:])
            prefix = jnp.transpose(state_pool[o].astype(F32), (1, 0, 2))
            xs, h_s = _pool_sample(xs, prefix.reshape((W_MAX - 1) * n_bs, D_MODEL), g_pre, wp, ps, g_post, n_bs, t_s)
            full = jnp.concatenate([prefix, h_s.reshape(t_s, n_bs, D_MODEL)], axis=0)
            pool_s.append(jnp.transpose(full[-(W_MAX - 1):], (1, 0, 2)))
        gfp, gfo = g_ffn_pre[l][None], g_ffn_post[l][None]
        xp = _ffn(xp, gfp, w_up, w_down, gfo, l, ROW_TILE)
        xs = _ffn(xs, gfp, w_up, w_down, gfo, l, tm_s)

    y_prompt = xp.reshape(n_bp, t_p, D_MODEL)[:, N_META:]
    y_sample = jnp.transpose(xs.reshape(t_s, n_bs, D_MODEL), (1, 0, 2))
    return (y_prompt, y_sample, jnp.stack(ckv_p), jnp.stack(kpe_p), jnp.stack(wkv_p), jnp.stack(shift_p),
            jnp.stack(pool_p), jnp.stack(ckv_s), jnp.stack(kpe_s), jnp.stack(wkv_s), jnp.stack(shift_s),
            jnp.stack(pool_s))
```

```python
import functools

import jax
import jax.numpy as jnp
from jax import lax
from jax.experimental import pallas as pl
from jax.experimental.pallas import tpu as pltpu

F32 = jnp.float32
BF16 = jnp.bfloat16

D_MODEL = 1024
N_META = 16
PAGE_SIZE = 128
H_A = 8
Q_LORA = 384
KV_LORA = 256
NOPE_DIM = 64
ROPE_DIM = 32
V_DIM = 64
ROPE_BASE = 10000.0
SM_SCALE = (NOPE_DIM + ROPE_DIM) ** -0.5
LOG2_E = 1.4426950408889634
H_B = 8
N_B = 64
D_B = H_B * N_B
DECAY_LORA = 64
AAA_LORA = 64
GATE_LORA = 160
RWKV_IN = 3 * D_B + DECAY_LORA + AAA_LORA + GATE_LORA
MLA_IN = Q_LORA + KV_LORA + ROPE_DIM
GN_EPS = 64e-5
POOL_WINDOWS = (2, 4, 8, 16)
POOL_GC = D_MODEL // len(POOL_WINDOWS)
W_MAX = 16
D_FF = 4 * D_MODEL
RMS_EPS = 1e-6

LANES = 128
QK_PAD = KV_LORA + LANES
VMEM_LIMIT = 56 * 1024 * 1024
ROW_TILE = 384
ATT_Q_TILE = 256
POOL_TILE = 688
PREP_TILE = 344
SCAN_CHUNK_P = 48
SCAN_CHUNK_S = 8
SCAN_SEQS_P = 4
SCAN_SEQS_S = 8
NEW_KEY_ROWS = 16
PAGES_PER_CHUNK = 8
FFN_STAGE_BYTES = 2 * 1024 * 1024


def _cparams(sem):
    return pltpu.CompilerParams(dimension_semantics=sem, vmem_limit_bytes=VMEM_LIMIT)


def _rms(x, g):
    return x * lax.rsqrt(jnp.mean(x * x, axis=-1, keepdims=True) + RMS_EPS) * g


def _dot(a, b):
    return jnp.dot(a, b, preferred_element_type=F32)


def _dot_nt(a, b):
    return lax.dot_general(a, b, (((1,), (1,)), ((), ())), preferred_element_type=F32)


def _dot_tn(a, b):
    return lax.dot_general(a, b, (((0,), (0,)), ((), ())), preferred_element_type=F32)


def _full(shape):
    n = len(shape)
    return pl.BlockSpec(shape, lambda *_: (0,) * n)


def _ab_in_kernel(x_ref, gpre_ref, wq2_ref, wkv_ref, wb_ref, gq_ref, gkv_ref, wuq_ref, wuk_ref, cos_ref, sin_ref,
                  ckv_ref, kpe_ref, zb_ref, qabs_ref, kvb_ref, *, period):
    h = _rms(x_ref[...], gpre_ref[...]).astype(BF16)
    zq2 = _dot(h, wq2_ref[...])
    zkv = _dot(h, wkv_ref[...])
    zb_ref[...] = _dot(h, wb_ref[...])
    ckv = _rms(zkv, gkv_ref[...])
    ckv_ref[...] = ckv
    tm = x_ref.shape[0]
    start = pl.multiple_of((pl.program_id(0) * tm) % period, 8)
    cos = cos_ref[pl.ds(start, tm), :]
    sin = sin_ref[pl.ds(start, tm), :]
    pe = zq2[:, Q_LORA:Q_LORA + ROPE_DIM]
    pe_sw = zq2[:, Q_LORA + ROPE_DIM:Q_LORA + 2 * ROPE_DIM]
    kpe = pe * cos[:, :ROPE_DIM] + pe_sw * sin[:, :ROPE_DIM]
    kpe_ref[...] = kpe
    kvb_ref[:, :KV_LORA] = ckv.astype(BF16)
    kvb_ref[:, KV_LORA:] = jnp.concatenate([kpe, jnp.zeros((kpe.shape[0], LANES - ROPE_DIM), F32)], axis=1).astype(BF16)
    qn = _rms(zq2[:, :Q_LORA], gq_ref[...]).astype(BF16)
    q2 = _dot(qn, wuq_ref[...])
    n_nope = H_A * NOPE_DIM
    for hd in range(H_A):
        qn_h = q2[:, hd * NOPE_DIM:(hd + 1) * NOPE_DIM].astype(BF16)
        qabs_ref[hd, :, 0:KV_LORA] = _dot(qn_h, wuk_ref[hd]).astype(BF16)
        r0 = n_nope + hd * LANES
        r1 = n_nope + (H_A + hd) * LANES
        qabs_ref[hd, :, KV_LORA:QK_PAD] = (q2[:, r0:r0 + LANES] * cos + q2[:, r1:r1 + LANES] * sin).astype(BF16)


def _ab_in_proj(x, g_pre, wts, cos_t, sin_t, tm):
    m = x.shape[0]
    wq2, wkv, wb, g_q, g_kv, wuq, wuk = wts
    row = lambda w: pl.BlockSpec((tm, w), lambda i: (i, 0))
    return pl.pallas_call(
        functools.partial(_ab_in_kernel, period=cos_t.shape[0] - tm),
        grid=(m // tm,),
        in_specs=[row(D_MODEL), _full(g_pre.shape), _full(wq2.shape), _full(wkv.shape), _full(wb.shape),
                  _full(g_q.shape), _full(g_kv.shape), _full(wuq.shape), _full(wuk.shape), _full(cos_t.shape),
                  _full(sin_t.shape)],
        out_specs=[row(KV_LORA), row(ROPE_DIM), row(RWKV_IN),
                   pl.BlockSpec((H_A, tm, QK_PAD), lambda i: (0, i, 0)), row(QK_PAD)],
        out_shape=[jax.ShapeDtypeStruct((m, KV_LORA), F32), jax.ShapeDtypeStruct((m, ROPE_DIM), F32),
                   jax.ShapeDtypeStruct((m, RWKV_IN), F32), jax.ShapeDtypeStruct((H_A, m, QK_PAD), BF16),
                   jax.ShapeDtypeStruct((m, QK_PAD), BF16)],
        compiler_params=_cparams(("parallel",)),
        name="ab_in_proj",
    )(x, g_pre, wq2, wkv, wb, g_q, g_kv, wuq, wuk, cos_t, sin_t)


def _attn_prompt_kernel(q_ref, kv_ref, o_ref, kpad, *, t_p, tq):
    t_pad = kpad.shape[0]

    @pl.when(pl.program_id(1) == 0)
    def _():
        kpad[:t_p, :] = kv_ref[0]
        kpad[t_p:, :] = jnp.zeros((t_pad - t_p, QK_PAD), BF16)

    n_t = t_p // tq
    for i in range(n_t):
        q0 = i * tq
        tq = t_p - q0 if i == n_t - 1 else tq
        kext = min(-(-(q0 + tq) // LANES) * LANES, t_pad)
        lo = (q0 // LANES) * LANES
        k = kpad[:kext, :]
        s = _dot_nt(q_ref[0, q0:q0 + tq, :], k)
        q_pos = q0 + lax.broadcasted_iota(jnp.int32, (tq, kext - lo), 0)
        k_pos = lo + lax.broadcasted_iota(jnp.int32, (tq, kext - lo), 1)
        s_diag = jnp.where(k_pos <= q_pos, s[:, lo:], -jnp.inf)
        s = jnp.concatenate([s[:, :lo], s_diag], axis=1) if lo else s_diag
        p = jnp.exp2((s - jnp.max(s, axis=-1, keepdims=True)) * (SM_SCALE * LOG2_E))
        l = jnp.sum(p, axis=-1, keepdims=True)
        o = _dot(p.astype(BF16), k[:, :KV_LORA])
        o_ref[0, q0:q0 + tq, :] = (o / l).astype(BF16)


def _attn_prompt(qabs, kvb, n_b, t_p):
    m = qabs.shape[1]
    t_pad = -(-t_p // LANES) * LANES
    return pl.pallas_call(
        functools.partial(_attn_prompt_kernel, t_p=t_p, tq=ATT_Q_TILE),
        grid=(n_b, H_A),
        in_specs=[pl.BlockSpec((1, t_p, QK_PAD), lambda b, h: (h, b, 0)),
                  pl.BlockSpec((1, t_p, QK_PAD), lambda b, h: (b, 0, 0))],
        out_specs=pl.BlockSpec((1, t_p, KV_LORA), lambda b, h: (h, b, 0)),
        out_shape=jax.ShapeDtypeStruct((H_A, m, KV_LORA), BF16),
        scratch_shapes=[pltpu.VMEM((t_pad, QK_PAD), BF16)],
        compiler_params=_cparams(("parallel", "arbitrary")),
        name="attn_prompt",
    )(qabs, kvb.reshape(n_b, t_p, QK_PAD))


def _attn_sample_kernel(pt_ref, q_ref, knew_ref, ckv_hbm, kpe_hbm, o_ref, ckv_buf, kpe_buf, s_scr, sems, *, layer,
                        n_pages, t_s):
    b = pl.program_id(0)
    n_b = pl.num_programs(0)
    slot = b % 2

    def page_copies(bb, sl, p):
        page = pt_ref[bb, p]
        return (pltpu.make_async_copy(ckv_hbm.at[layer, page], ckv_buf.at[sl, p], sems.at[0, sl]),
                pltpu.make_async_copy(kpe_hbm.at[layer, page], kpe_buf.at[sl, p], sems.at[1, sl]))

    def start_all(bb, sl):
        for p in range(n_pages):
            for cp in page_copies(bb, sl, p):
                cp.start()

    @pl.when(b == 0)
    def _():
        start_all(0, 0)

    @pl.when(b + 1 < n_b)
    def _():
        start_all(b + 1, 1 - slot)

    for p in range(n_pages):
        for cp in page_copies(b, slot, p):
            cp.wait()

    q = q_ref[0]
    q_lat = q[:, :KV_LORA]
    q_pe = q[:, KV_LORA:KV_LORA + ROPE_DIM]
    rows = PAGES_PER_CHUNK * PAGE_SIZE
    n_chunks = n_pages // PAGES_PER_CHUNK

    def latent_rows(c):
        return ckv_buf[slot, c * PAGES_PER_CHUNK:(c + 1) * PAGES_PER_CHUNK].reshape(rows, KV_LORA).astype(BF16)

    for c in range(n_chunks):
        kp = jnp.concatenate([kpe_buf[slot, c * PAGES_PER_CHUNK + i] for i in range(PAGES_PER_CHUNK)], axis=1)
        s_scr[:, c * rows:(c + 1) * rows] = (_dot_nt(q_lat, latent_rows(c)) + _dot(q_pe, kp.astype(BF16))) * SM_SCALE
    kn = knew_ref[0]
    s_new = _dot_nt(q, kn) * SM_SCALE
    step = lax.broadcasted_iota(jnp.int32, s_new.shape, 0) % t_s
    key = lax.broadcasted_iota(jnp.int32, s_new.shape, 1)
    s_new = jnp.where(key <= step, s_new, -jnp.inf)
    mx = s_scr[:, 0:rows]
    for c in range(1, n_chunks):
        mx = jnp.maximum(mx, s_scr[:, c * rows:(c + 1) * rows])
    m = jnp.maximum(jnp.max(mx, axis=-1, keepdims=True), jnp.max(s_new, axis=-1, keepdims=True))
    p_new = jnp.exp(s_new - m)
    acc = _dot(p_new.astype(BF16), kn[:, :KV_LORA])
    p_sum = jnp.zeros((q.shape[0], rows), F32)
    for c in range(n_chunks):
        p = jnp.exp(s_scr[:, c * rows:(c + 1) * rows] - m)
        p_sum = p_sum + p
        acc = acc + _dot(p.astype(BF16), latent_rows(c))
    l = jnp.sum(p_sum, axis=-1, keepdims=True) + jnp.sum(p_new, axis=-1, keepdims=True)
    o_ref[0] = (acc / l).astype(BF16)


def _attn_sample(page_table, q_s, k_new, cache_ckv, cache_kpe, layer, t_s):
    n_b, n_pages = page_table.shape
    n_q = q_s.shape[1]
    grid_spec = pltpu.PrefetchScalarGridSpec(
        num_scalar_prefetch=1,
        grid=(n_b,),
        in_specs=[pl.BlockSpec((1, n_q, QK_PAD), lambda b, pt: (b, 0, 0)),
                  pl.BlockSpec((1,) + k_new.shape[1:], lambda b, pt: (b, 0, 0)),
                  pl.BlockSpec(memory_space=pl.ANY),
                  pl.BlockSpec(memory_space=pl.ANY)],
        out_specs=pl.BlockSpec((1, n_q, KV_LORA), lambda b, pt: (b, 0, 0)),
        scratch_shapes=[pltpu.VMEM((2, n_pages, PAGE_SIZE, KV_LORA), F32),
                        pltpu.VMEM((2, n_pages, ROPE_DIM, PAGE_SIZE), F32),
                        pltpu.VMEM((n_q, n_pages * PAGE_SIZE), F32),
                        pltpu.SemaphoreType.DMA((2, 2))],
    )
    return pl.pallas_call(
        functools.partial(_attn_sample_kernel, layer=layer, n_pages=n_pages, t_s=t_s),
        grid_spec=grid_spec,
        out_shape=jax.ShapeDtypeStruct((n_b, n_q, KV_LORA), BF16),
        compiler_params=_cparams(("arbitrary",)),
        name="attn_sample",
    )(page_table, q_s, k_new, cache_ckv, cache_kpe)


def _ab_out_kernel(olat_ref, ob_ref, x_ref, wuv_ref, wout_ref, gpost_ref, o_ref, cat_ref):
    n_a = H_A * V_DIM
    for hd in range(0, H_A, 2):
        pair = [_dot(olat_ref[hd + i], wuv_ref[hd + i]) for i in range(2)]
        cat_ref[:, hd * V_DIM:(hd + 2) * V_DIM] = jnp.concatenate(pair, axis=1).astype(BF16)
    cat_ref[:, n_a:] = ob_ref[...].astype(BF16)
    y = _dot(cat_ref[...], wout_ref[...])
    o_ref[...] = x_ref[...] + _rms(y, gpost_ref[...])


def _ab_out_proj(olat, o_b, x, wuv, wout, g_post, tm):
    m = x.shape[0]
    row = lambda w: pl.BlockSpec((tm, w), lambda i: (i, 0))
    return pl.pallas_call(
        _ab_out_kernel,
        grid=(m // tm,),
        in_specs=[pl.BlockSpec((H_A, tm, KV_LORA), lambda i: (0, i, 0)), row(D_B), row(D_MODEL),
                  _full(wuv.shape), _full(wout.shape), _full(g_post.shape)],
        out_specs=row(D_MODEL),
        out_shape=jax.ShapeDtypeStruct((m, D_MODEL), F32),
        scratch_shapes=[pltpu.VMEM((tm, H_A * V_DIM + D_B), BF16)],
        compiler_params=_cparams(("parallel",)),
        name="ab_out_proj",
    )(olat, o_b, x, wuv, wout, g_post)


def _ffn_kernel(xp_ref, xs_ref, gpre_ref, wup_hbm, wdown_hbm, gpost_ref, op_ref, os_ref, wup_ref, wdown_ref, st_up,
                st_dn, sems, *, ff_chunk, layer, n_p):
    @pl.when(pl.program_id(0) == 0)
    def _():
        jobs = []
        for k, (src, dst, st) in enumerate(((wup_hbm, wup_ref, st_up), (wdown_hbm, wdown_ref, st_dn))):
            rows = st.shape[1]
            for c in range(dst.shape[0] // rows):
                jobs.append((k, c, rows, src, dst, st))

        def copy(job, slot):
            k, c, rows, src, _, st = job
            return pltpu.make_async_copy(src.at[layer, c * rows:(c + 1) * rows], st.at[slot], sems.at[k, slot])

        copy(jobs[0], 0).start()
        for i, job in enumerate(jobs):
            slot = i % 2
            if i + 1 < len(jobs):
                copy(jobs[i + 1], 1 - slot).start()
            copy(job, slot).wait()
            k, c, rows, _, dst, st = job
            dst[c * rows:(c + 1) * rows, :] = st[slot].astype(BF16)

    def mlp(x_ref, o_ref):
        x = x_ref[...]
        h = _rms(x, gpre_ref[...]).astype(BF16)
        y = jnp.zeros(x.shape, F32)
        for c in range(D_FF // ff_chunk):
            u = _dot(h, wup_ref[:, c * ff_chunk:(c + 1) * ff_chunk])
            a = jnp.square(jnp.maximum(u, 0.0)).astype(BF16)
            y = y + _dot(a, wdown_ref[c * ff_chunk:(c + 1) * ff_chunk, :])
        o_ref[...] = x + _rms(y, gpost_ref[...])

    @pl.when(pl.program_id(0) < n_p)
    def _():
        mlp(xp_ref, op_ref)

    @pl.when(pl.program_id(0) == n_p)
    def _():
        mlp(xs_ref, os_ref)


def _ffn(xp, xs, g_pre, w_up, w_down, g_post, layer, tm):
    n_p = xp.shape[0] // tm
    row = pl.BlockSpec((tm, D_MODEL), lambda i: (jnp.minimum(i, n_p - 1), 0))
    return pl.pallas_call(
        functools.partial(_ffn_kernel, ff_chunk=1024, layer=layer, n_p=n_p),
        grid=(n_p + 1,),
        in_specs=[row, _full(xs.shape), _full(g_pre.shape), pl.BlockSpec(memory_space=pl.ANY),
                  pl.BlockSpec(memory_space=pl.ANY), _full(g_post.shape)],
        out_specs=[row, _full(xs.shape)],
        out_shape=[jax.ShapeDtypeStruct(xp.shape, F32), jax.ShapeDtypeStruct(xs.shape, F32)],
        scratch_shapes=[pltpu.VMEM(w_up.shape[1:], BF16), pltpu.VMEM(w_down.shape[1:], BF16),
                        pltpu.VMEM((2, FFN_STAGE_BYTES // (4 * D_FF), D_FF), F32),
                        pltpu.VMEM((2, FFN_STAGE_BYTES // (4 * D_MODEL), D_MODEL), F32),
                        pltpu.SemaphoreType.DMA((2, 2))],
        compiler_params=_cparams(("arbitrary",)),
        name="ffn",
    )(xp, xs, g_pre, w_up, w_down, g_post)


def _split_dot(x, w):
    hi = x.astype(BF16)
    lo = (x - hi.astype(F32)).astype(BF16)
    return _dot(hi, w) + _dot(lo, w)


def _rwkv_prep_kernel(zb_ref, first_ref, mu_ref, w0_ref, w2_ref, a0_ref, a2_ref, g2_ref, kk_ref, ka_ref, rk_ref,
                      ones_ref, r_o, lw_o, keff_o, v_o, kn_o, b_o, g_o, bonus_o, carry, *, shift):
    @pl.when(pl.program_id(1) == 0)
    def _():
        carry[...] = first_ref[0]

    zb = zb_ref[...]
    tm = zb.shape[0]
    if shift == 1:
        row = lax.broadcasted_iota(jnp.int32, (tm, 1), 0)
        prev = jnp.where(row == 0, carry[0:1, :], pltpu.roll(zb, 1, 0))
        carry[0:1, :] = zb[tm - 1:tm, :]
    else:
        prev = jnp.concatenate([carry[...], zb[:tm - shift]], axis=0)
        carry[...] = zb[tm - shift:]
    xm = zb + (prev - zb) * mu_ref[...]
    o3 = 3 * D_B
    o4 = o3 + DECAY_LORA
    o5 = o4 + AAA_LORA
    r = xm[:, :D_B]
    k = xm[:, D_B:2 * D_B]
    v = xm[:, 2 * D_B:o3]
    z = -(w0_ref[...] + _dot(jnp.tanh(xm[:, o3:o4]).astype(BF16), w2_ref[...].astype(BF16)))
    softplus = jnp.maximum(z, 0.0) + jnp.log(1.0 + jnp.exp(-jnp.abs(z)))
    lw_o[...] = -jnp.exp(-softplus - 0.5)
    a = 1.0 / (1.0 + jnp.exp(-(a0_ref[...] + _dot(xm[:, o4:o5].astype(BF16), a2_ref[...].astype(BF16)))))
    sg = 1.0 / (1.0 + jnp.exp(-xm[:, o5:]))
    g_o[...] = _dot(sg.astype(BF16), g2_ref[...].astype(BF16))
    ones = ones_ref[...]
    kk = k * kk_ref[...]
    kk = kk / jnp.maximum(jnp.sqrt(_split_dot(kk * kk, ones)), 1e-12)
    keff = k * (1.0 + (a - 1.0) * ka_ref[...])
    r_o[...] = r
    keff_o[...] = keff
    v_o[...] = v
    kn_o[...] = kk
    b_o[...] = kk * a
    bonus_o[...] = _split_dot(r * keff * rk_ref[...], ones) * v


def _rwkv_prep(zb, first, rw, tm, shift):
    m = zb.shape[0]
    n_seq = first.shape[0]
    n_t = m // (n_seq * tm)
    row = lambda w: pl.BlockSpec((tm, w), lambda b, j: (b * n_t + j, 0))
    return pl.pallas_call(
        functools.partial(_rwkv_prep_kernel, shift=shift),
        grid=(n_seq, n_t),
        in_specs=[row(RWKV_IN), pl.BlockSpec((1,) + first.shape[1:], lambda b, j: (b, 0, 0))]
        + [_full(w.shape) for w in rw],
        out_specs=[row(D_B)] * 8,
        out_shape=[jax.ShapeDtypeStruct((m, D_B), F32)] * 8,
        scratch_shapes=[pltpu.VMEM(first.shape[1:], F32)],
        compiler_params=_cparams(("parallel", "arbitrary")),
        name="rwkv_prep",
    )(zb, first, *rw)


def _rwkv_scan_kernel(r_ref, lw_ref, keff_ref, v_ref, kn_ref, b_ref, g_ref, bonus_ref, lnw_ref, lnb_ref, s0_ref,
                      tri_ref, o_ref, s_out_ref, s_scr, *, chunk, n_sb):
    c = pl.program_id(1)
    n_g = H_B // 4
    gw = 4 * N_B
    hw = 4 * chunk
    groups = [(j, g) for j in range(n_sb) for g in range(n_g)]

    @pl.when(c == 0)
    def _():
        for j, g in groups:
            s_scr[j, g] = jnp.concatenate([s0_ref[j, 4 * g + i] for i in range(4)], axis=1)

    def blk(idx, size):
        return ((idx >= size).astype(jnp.int32) + (idx >= 2 * size).astype(jnp.int32)
                + (idx >= 3 * size).astype(jnp.int32))

    t_i = lax.broadcasted_iota(jnp.int32, (chunk, hw), 0)
    col = lax.broadcasted_iota(jnp.int32, (chunk, hw), 1)
    s_i = col - chunk * blk(col, chunk)
    strict = s_i < t_i
    incl = s_i <= t_i
    eye = jnp.where(s_i == t_i, 1.0, 0.0).astype(F32)
    levels = []
    k_lvl = 1
    while (1 << (k_lvl - 1)) < chunk:
        same_new = (t_i >> k_lvl) == (s_i >> k_lvl)
        same_old = (t_i >> (k_lvl - 1)) == (s_i >> (k_lvl - 1))
        levels.append(strict & same_new & jnp.logical_not(same_old))
        k_lvl += 1
    bd_hh = (blk(lax.broadcasted_iota(jnp.int32, (hw, hw), 0), chunk)
             == blk(lax.broadcasted_iota(jnp.int32, (hw, hw), 1), chunk))
    bd_hg = (blk(lax.broadcasted_iota(jnp.int32, (hw, gw), 0), chunk)
             == (lax.broadcasted_iota(jnp.int32, (hw, gw), 1) >> 6))
    lane_head = lax.broadcasted_iota(jnp.int32, (N_B, gw), 1) >> 6
    bd_gg = (lax.broadcasted_iota(jnp.int32, (gw, gw), 0) >> 6) == (lax.broadcasted_iota(jnp.int32, (gw, gw), 1) >> 6)
    ones_gg = jnp.where(bd_gg, 1.0, 0.0).astype(BF16)

    def bdiag(x, mask):
        return jnp.where(mask, jnp.concatenate([x] * 4, axis=0), 0.0).astype(BF16)

    tri = tri_ref[...]
    qr, bk_d, kk_d, v_d, kq, rq, bke, tkv, g_last = {}, {}, {}, {}, {}, {}, {}, {}, {}
    for j in range(n_sb):
        lw = lw_ref[j]
        hi = lw.astype(BF16)
        r1 = lw - hi.astype(F32)
        mid = r1.astype(BF16)
        lo = (r1 - mid.astype(F32)).astype(BF16)
        cs = _dot(tri, hi) + _dot(tri, mid) + _dot(tri, lo)
        cs_last = cs[chunk - 1:chunk, :]
        g_inv = jnp.exp(-cs)
        g_end = jnp.exp(cs_last - cs)
        kq_j = kn_ref[j] * jnp.exp(cs - lw)
        rq_j = r_ref[j] * jnp.exp(cs)
        b_j = b_ref[j]
        ke_j = keff_ref[j]
        v_j = v_ref[j]
        bk_j = b_j * g_inv
        kk_j = ke_j * g_inv
        bke_j = b_j * g_end
        kke_j = ke_j * g_end
        gl_j = jnp.exp(cs_last)
        for g in range(n_g):
            sl = slice(g * gw, (g + 1) * gw)
            gr = (j, g)
            kq[gr], rq[gr], g_last[gr] = kq_j[:, sl], rq_j[:, sl], gl_j[:, sl]
            qr[gr] = jnp.concatenate([kq[gr], rq[gr]], axis=0).astype(BF16)
            bk_d[gr], kk_d[gr], v_d[gr] = bdiag(bk_j[:, sl], bd_hg), bdiag(kk_j[:, sl], bd_hg), bdiag(v_j[:, sl], bd_hg)
            bke[gr] = bke_j[:, sl]
            tkv[gr] = (v_j[:, sl], kke_j[:, sl])

    p1 = {gr: _dot_nt(qr[gr], bk_d[gr]) for gr in groups}
    p2 = {gr: _dot_nt(qr[gr], kk_d[gr]) for gr in groups}
    a_m = {gr: -p1[gr][:chunk] for gr in groups}
    t_m = {gr: eye + jnp.where(levels[0], a_m[gr], 0.0) for gr in groups}
    w_m = {gr: _dot(jnp.where(strict, p2[gr][:chunk], 0.0).astype(BF16), v_d[gr]) for gr in groups}
    for lvl in levels[1:]:
        x_m = {gr: _dot(jnp.where(lvl, a_m[gr], 0.0).astype(BF16), bdiag(t_m[gr], bd_hh)) for gr in groups}
        t_m = {gr: t_m[gr] + _dot(t_m[gr].astype(BF16), bdiag(x_m[gr], bd_hh)) for gr in groups}
    t_b = {gr: t_m[gr].astype(BF16) for gr in groups}
    kq2 = {gr: _dot(t_b[gr], bdiag(kq[gr], bd_hg)) for gr in groups}
    w2 = {gr: _dot(t_b[gr], bdiag(w_m[gr], bd_hg)) for gr in groups}
    ar = {gr: jnp.where(incl, -p1[gr][chunk:], 0.0).astype(BF16) for gr in groups}
    br = {gr: jnp.where(incl, p2[gr][chunk:], 0.0).astype(BF16) for gr in groups}
    rq2 = {gr: rq[gr] + _dot(ar[gr], bdiag(kq2[gr], bd_hg)) for gr in groups}
    y0 = {gr: _dot(ar[gr], bdiag(w2[gr], bd_hg)) + _dot(br[gr], v_d[gr]) for gr in groups}
    bke_b = {gr: bke[gr].astype(BF16) for gr in groups}
    m_full = {gr: _dot_tn(kq2[gr].astype(BF16), bke_b[gr]) for gr in groups}
    n_full = {gr: _dot_tn(jnp.concatenate([tkv[gr][0], -w2[gr]], axis=0).astype(BF16),
                          jnp.concatenate([tkv[gr][1], bke[gr]], axis=0).astype(BF16)) for gr in groups}

    s_old = {gr: s_scr[gr[0], gr[1]] for gr in groups}
    s_d = {gr: bdiag(s_old[gr], bd_gg) for gr in groups}
    y = {gr: _dot_nt(rq2[gr].astype(BF16), s_d[gr]) + y0[gr] for gr in groups}
    s_new = {}
    for gr in groups:
        n_h = jnp.zeros((N_B, gw), F32)
        for i in range(4):
            n_h = n_h + jnp.where(lane_head == i, n_full[gr][i * N_B:(i + 1) * N_B, :], 0.0)
        s_new[gr] = (s_old[gr] * g_last[gr]
                     - _dot(s_old[gr].astype(BF16), jnp.where(bd_gg, m_full[gr], 0.0).astype(BF16)) + n_h)
        s_scr[gr[0], gr[1]] = s_new[gr]

    @pl.when(c == pl.num_programs(1) - 1)
    def _():
        for j, g in groups:
            for i in range(4):
                s_out_ref[j, 4 * g + i] = s_new[(j, g)][:, i * N_B:(i + 1) * N_B]

    inv_n = 1.0 / N_B
    for j in range(n_sb):
        outs = []
        for g in range(n_g):
            y_g = y[(j, g)]
            mean = _split_dot(y_g, ones_gg) * inv_n
            d = y_g - mean
            var = _split_dot(d * d, ones_gg) * inv_n
            outs.append(d * lax.rsqrt(var + GN_EPS))
        yn = jnp.concatenate(outs, axis=-1) * lnw_ref[...] + lnb_ref[...]
        o_ref[j] = (yn + bonus_ref[j]) * g_ref[j]


def _rwkv_scan(streams, ln_w, ln_b, s0, layer, chunk, n_sb):
    n_seq, t, _ = streams[0].shape
    n_chunks = t // chunk
    tri = jnp.tril(jnp.ones((chunk, chunk), F32)).astype(BF16)
    row = pl.BlockSpec((n_sb, chunk, D_B), lambda b, c: (b, c, 0))
    st = pl.BlockSpec((n_sb, H_B, N_B, N_B), lambda b, c: (b, 0, 0, 0))
    st_in = pl.BlockSpec((None, n_sb, H_B, N_B, N_B), lambda b, c: (layer, b, 0, 0, 0))
    return pl.pallas_call(
        functools.partial(_rwkv_scan_kernel, chunk=chunk, n_sb=n_sb),
        grid=(n_seq // n_sb, n_chunks),
        in_specs=[row] * 8 + [_full(ln_w.shape), _full(ln_b.shape), st_in, _full(tri.shape)],
        out_specs=[row, st],
        out_shape=[jax.ShapeDtypeStruct((n_seq, t, D_B), F32), jax.ShapeDtypeStruct(s0.shape[1:], F32)],
        scratch_shapes=[pltpu.VMEM((n_sb, H_B // 4, N_B, 4 * N_B), F32)],
        compiler_params=_cparams(("parallel", "arbitrary")),
        name="rwkv_scan",
    )(*streams, ln_w, ln_b, s0, tri)


def _pool_prompt_kernel(x_ref, gpre_ref, wpool_ref, pscale_ref, gpost_ref, o_ref, tail_ref, hext, *, tm):
    j = pl.program_id(1)
    x = x_ref[...]
    h = _rms(x, gpre_ref[...])

    @pl.when(j == 0)
    def _():
        hext[0:W_MAX, :] = jnp.zeros((W_MAX, D_MODEL), F32)

    @pl.when(j > 0)
    def _():
        hext[0:W_MAX, :] = hext[tm:tm + W_MAX, :]

    hext[W_MAX:, :] = h
    tail_ref[0] = h[tm - W_MAX:, :]
    pos = j * tm + lax.broadcasted_iota(jnp.int32, (tm, 1), 0)
    acc = hext[...]
    ys = []
    for gi, w in enumerate(POOL_WINDOWS):
        acc = acc[:, POOL_GC * (1 if gi else 0):]
        acc = acc + pltpu.roll(acc, w // 2, 0)
        cnt = jnp.minimum(pos + 1, w).astype(F32)
        pooled = acc[W_MAX:, :POOL_GC] / cnt - h[:, gi * POOL_GC:(gi + 1) * POOL_GC]
        ys.append(_dot(pooled.astype(BF16), wpool_ref[gi]))
    y = jnp.concatenate(ys, axis=-1) * pscale_ref[...]
    o_ref[...] = x + _rms(y, gpost_ref[...])


def _pool_prompt(x, g_pre, w_pool, p_scale, g_post, n_b, t_p):
    m = x.shape[0]
    tm = POOL_TILE
    n_t = t_p // tm
    row = pl.BlockSpec((tm, D_MODEL), lambda b, j: (b * n_t + j, 0))
    return pl.pallas_call(
        functools.partial(_pool_prompt_kernel, tm=tm),
        grid=(n_b, n_t),
        in_specs=[row, _full(g_pre.shape), _full(w_pool.shape), _full(p_scale.shape), _full(g_post.shape)],
        out_specs=[row, pl.BlockSpec((1, W_MAX, D_MODEL), lambda b, j: (b, 0, 0))],
        out_shape=[jax.ShapeDtypeStruct((m, D_MODEL), F32), jax.ShapeDtypeStruct((n_b, W_MAX, D_MODEL), F32)],
        scratch_shapes=[pltpu.VMEM((W_MAX + tm, D_MODEL), F32)],
        compiler_params=_cparams(("parallel", "arbitrary")),
        name="pool_prompt",
    )(x, g_pre, w_pool, p_scale, g_post)


def _pool_sample_kernel(x_ref, pre_ref, gpre_ref, wpool_ref, pscale_ref, gpost_ref, o_ref, h_ref, *, n_b, t_s):
    x = x_ref[...]
    h = _rms(x, gpre_ref[...])
    h_ref[...] = h
    n_pre = W_MAX - 1
    rows = [pre_ref[i * n_b:(i + 1) * n_b, :] for i in range(n_pre)] + [h[t * n_b:(t + 1) * n_b, :] for t in range(t_s)]
    outs = []
    for t in range(t_s):
        ys = []
        for gi, w in enumerate(POOL_WINDOWS):
            sl = slice(gi * POOL_GC, (gi + 1) * POOL_GC)
            win = rows[n_pre + t][:, sl]
            for d in range(1, w):
                win = win + rows[n_pre + t - d][:, sl]
            pooled = win / float(w) - rows[n_pre + t][:, sl]
            ys.append(_dot(pooled.astype(BF16), wpool_ref[gi]))
        outs.append(jnp.concatenate(ys, axis=-1))
    y = jnp.concatenate(outs, axis=0) * pscale_ref[...]
    o_ref[...] = x + _rms(y, gpost_ref[...])


def _pool_sample(x, prefix, g_pre, w_pool, p_scale, g_post, n_b, t_s):
    args = (x, prefix, g_pre, w_pool, p_scale, g_post)
    return pl.pallas_call(
        functools.partial(_pool_sample_kernel, n_b=n_b, t_s=t_s),
        grid=(1,),
        in_specs=[_full(a.shape) for a in args],
        out_specs=[_full(x.shape), _full(x.shape)],
        out_shape=[jax.ShapeDtypeStruct(x.shape, F32)] * 2,
        compiler_params=_cparams(("arbitrary",)),
        name="pool_sample",
    )(*args)


def _rope_tables(pos):
    half = ROPE_DIM // 2
    inv = ROPE_BASE ** (-jnp.arange(half, dtype=F32) / half)
    ang = pos[:, None] * inv[None, :]
    cos, sin = jnp.cos(ang), jnp.sin(ang)
    reps = LANES // ROPE_DIM
    return jnp.tile(jnp.concatenate([cos, cos], -1), (1, reps)), jnp.tile(jnp.concatenate([-sin, sin], -1), (1, reps))


def _swap_halves(w):
    half = ROPE_DIM // 2
    return jnp.concatenate([w[..., half:], w[..., :half]], axis=-1)


def _ab_weights(e, w_in, g_q, w_uq, g_kv, w_uk):
    w = w_in[e]
    w_pe = w[:, Q_LORA + KV_LORA:MLA_IN]
    wq2 = jnp.concatenate([w[:, :Q_LORA], w_pe, _swap_halves(w_pe)], axis=1).astype(BF16)
    wkv = w[:, Q_LORA:Q_LORA + KV_LORA].astype(BF16)
    wb = w[:, MLA_IN:].astype(BF16)
    uq = w_uq[e].reshape(Q_LORA, H_A, NOPE_DIM + ROPE_DIM)
    pad = ((0, 0), (0, 0), (0, LANES - ROPE_DIM))
    uq_pe = uq[:, :, NOPE_DIM:]
    wuq = jnp.concatenate([uq[:, :, :NOPE_DIM].reshape(Q_LORA, H_A * NOPE_DIM),
                           jnp.pad(uq_pe, pad).reshape(Q_LORA, H_A * LANES),
                           jnp.pad(_swap_halves(uq_pe), pad).reshape(Q_LORA, H_A * LANES)], axis=1).astype(BF16)
    wuk = jnp.transpose(w_uk[e], (1, 2, 0)).astype(BF16)
    return wq2, wkv, wb, g_q[e][None], g_kv[e][None], wuq, wuk


def _rwkv_weights(e, mu_shift, w0, w2, a0, a2, g2, k_k, k_a, r_k):
    head = jnp.arange(D_B) // N_B
    ones = (head[:, None] == head[None, :]).astype(BF16)
    return (mu_shift[e][None], w0[e][None], w2[e], a0[e][None], a2[e], g2[e],
            k_k[e][None], k_a[e][None], r_k[e].reshape(1, D_B), ones)


def kernel(x_prompt, x_sample, cache_ckv, cache_kpe, page_table, state_wkv, state_shift, state_pool, meta_tokens,
           g_mix_pre, g_mix_post, g_ffn_pre, g_ffn_post, w_in, g_q, w_uq, g_kv, w_uk, w_uv, mu_shift, w0, w2, a0, a2,
           g2, k_k, k_a, r_k, ln_w, ln_b, w_out, w_pool, pool_scale, w_up, w_down):
    n_bp, seq, _ = x_prompt.shape
    n_bs, t_s, _ = x_sample.shape
    depth = g_mix_pre.shape[0]
    t_p = seq + N_META
    n_pages = page_table.shape[1]
    past = n_pages * PAGE_SIZE
    m_p = n_bp * t_p
    m_s = n_bs * t_s

    meta = jnp.broadcast_to(meta_tokens[None].astype(x_prompt.dtype), (n_bp, N_META, D_MODEL))
    xp = jnp.concatenate([meta, x_prompt], axis=1).reshape(m_p, D_MODEL)
    xs = jnp.transpose(x_sample, (1, 0, 2)).reshape(m_s, D_MODEL)

    tm_s = m_s
    cos_p, sin_p = _rope_tables((jnp.arange(t_p + ROW_TILE) % t_p).astype(F32))
    cos_s, sin_s = _rope_tables((past + (jnp.arange(m_s + tm_s) % m_s) // n_bs).astype(F32))
    cache_kpe_t = jnp.swapaxes(cache_kpe, 2, 3)

    ckv_p, kpe_p, wkv_p, shift_p, pool_p = [], [], [], [], []
    ckv_s, kpe_s, wkv_s, shift_s, pool_s = [], [], [], [], []
    for l in range(depth):
        g_pre, g_post = g_mix_pre[l][None], g_mix_post[l][None]
        if l % 2 == 0:
            e = l // 2
            ab_w = _ab_weights(e, w_in, g_q, w_uq, g_kv, w_uk)
            rw = _rwkv_weights(e, mu_shift, w0, w2, a0, a2, g2, k_k, k_a, r_k)
            wuv = jnp.transpose(w_uv[e], (1, 0, 2)).astype(BF16)
            wout = w_out[e].astype(BF16)
            lnw, lnb = ln_w[e][None], ln_b[e][None]

            ckv, kpe, zb, qabs, kvb = _ab_in_proj(xp, g_pre, ab_w, cos_p, sin_p, ROW_TILE)
            olat = _attn_prompt(qabs, kvb, n_bp, t_p)
            zb3 = zb.reshape(n_bp, t_p, RWKV_IN)
            first = jnp.zeros((n_bp, 8, RWKV_IN), F32)
            streams = [a.reshape(n_bp, t_p, D_B) for a in _rwkv_prep(zb, first, rw, PREP_TILE, 1)]
            o_b, s_fin = _rwkv_scan(streams, lnw, lnb, jnp.zeros((1, n_bp, H_B, N_B, N_B), F32), 0, SCAN_CHUNK_P,
                                    SCAN_SEQS_P)
            xp = _ab_out_proj(olat, o_b.reshape(m_p, D_B), xp, wuv, wout, g_post, ROW_TILE)
            ckv_p.append(ckv.reshape(n_bp, t_p, KV_LORA))
            kpe_p.append(kpe.reshape(n_bp, t_p, ROPE_DIM))
            wkv_p.append(s_fin)
            shift_p.append(zb3[:, -1])

            ckv, kpe, zb, qabs, kvb = _ab_in_proj(xs, g_pre, ab_w, cos_s, sin_s, tm_s)
            q_s = jnp.transpose(qabs.reshape(H_A, t_s, n_bs, QK_PAD), (2, 0, 1, 3)).reshape(n_bs, H_A * t_s, QK_PAD)
            k_new = jnp.pad(jnp.transpose(kvb.reshape(t_s, n_bs, QK_PAD), (1, 0, 2)),
                            ((0, 0), (0, NEW_KEY_ROWS - t_s), (0, 0)))
            o_s = _attn_sample(page_table, q_s, k_new, cache_ckv, cache_kpe_t, e, t_s)
            olat = jnp.transpose(o_s.reshape(n_bs, H_A, t_s, KV_LORA), (1, 2, 0, 3)).reshape(H_A, m_s, KV_LORA)
            streams = _rwkv_prep(zb, state_shift[e].astype(F32)[None], rw, tm_s, n_bs)
            pad_t = lambda a: jnp.pad(jnp.transpose(a.reshape(t_s, n_bs, D_B), (1, 0, 2)),
                                      ((0, 0), (0, SCAN_CHUNK_S - t_s), (0, 0)))
            o_b, s_fin = _rwkv_scan([pad_t(a) for a in streams], lnw, lnb, state_wkv.astype(F32), e, SCAN_CHUNK_S,
                                    SCAN_SEQS_S)
            o_b = jnp.transpose(o_b[:, :t_s], (1, 0, 2)).reshape(m_s, D_B)
            xs = _ab_out_proj(olat, o_b, xs, wuv, wout, g_post, tm_s)
            ckv_s.append(jnp.transpose(ckv.reshape(t_s, n_bs, KV_LORA), (1, 0, 2)))
            kpe_s.append(jnp.transpose(kpe.reshape(t_s, n_bs, ROPE_DIM), (1, 0, 2)))
            wkv_s.append(s_fin)
            shift_s.append(zb[m_s - n_bs:])
        else:
            o = l // 2
            wp = w_pool[o].astype(BF16)
            ps = pool_scale[o][None]
            xp, tail = _pool_prompt(xp, g_pre, wp, ps, g_post, n_bp, t_p)
            pool_p.append(tail[:, 1:])
            prefix = jnp.transpose(state_pool[o].astype(F32), (1, 0, 2))
            xs, h_s = _pool_sample(xs, prefix.reshape((W_MAX - 1) * n_bs, D_MODEL), g_pre, wp, ps, g_post, n_bs, t_s)
            full = jnp.concatenate([prefix, h_s.reshape(t_s, n_bs, D_MODEL)], axis=0)
            pool_s.append(jnp.transpose(full[-(W_MAX - 1):], (1, 0, 2)))
        gfp, gfo = g_ffn_pre[l][None], g_ffn_post[l][None]
        xp, xs = _ffn(xp, xs, gfp, w_up, w_down, gfo, l, ROW_TILE)

    y_prompt = xp.reshape(n_bp, t_p, D_MODEL)[:, N_META:]
    y_sample = jnp.transpose(xs.reshape(t_s, n_bs, D_MODEL), (1, 0, 2))
    return (y_prompt, y_sample, jnp.stack(ckv_p), jnp.stack(kpe_p), jnp.stack(wkv_p), jnp.stack(shift_p),
            jnp.stack(pool_p), jnp.stack(ckv_s), jnp.stack(kpe_s), jnp.stack(wkv_s), jnp.stack(shift_s),
            jnp.stack(pool_s))
```

```python
import functools

import jax
import jax.numpy as jnp
from jax import lax
from jax.experimental import pallas as pl
from jax.experimental.pallas import tpu as pltpu

F32 = jnp.float32
BF16 = jnp.bfloat16

D_MODEL = 1024
N_META = 16
PAGE_SIZE = 128
H_A = 8
Q_LORA = 384
KV_LORA = 256
NOPE_DIM = 64
ROPE_DIM = 32
V_DIM = 64
ROPE_BASE = 10000.0
SM_SCALE = (NOPE_DIM + ROPE_DIM) ** -0.5
LOG2_E = 1.4426950408889634
H_B = 8
N_B = 64
D_B = H_B * N_B
DECAY_LORA = 64
AAA_LORA = 64
GATE_LORA = 160
RWKV_IN = 3 * D_B + DECAY_LORA + AAA_LORA + GATE_LORA
MLA_IN = Q_LORA + KV_LORA + ROPE_DIM
GN_EPS = 64e-5
POOL_WINDOWS = (2, 4, 8, 16)
POOL_GC = D_MODEL // len(POOL_WINDOWS)
W_MAX = 16
D_FF = 4 * D_MODEL
RMS_EPS = 1e-6

LANES = 128
QK_PAD = KV_LORA + LANES
VMEM_LIMIT = 56 * 1024 * 1024
ROW_TILE = 384
ATT_Q_TILE = 256
ATT_SCORE_LAG = 2
POOL_TILE = 688
PREP_TILE = 344
SCAN_CHUNK_P = 48
SCAN_CHUNK_S = 8
SCAN_SEQS_P = 4
SCAN_SEQS_S = 8
NEW_KEY_ROWS = 16
PAGES_PER_CHUNK = 8
FFN_STAGE_BYTES = 2 * 1024 * 1024


def _cparams(sem):
    return pltpu.CompilerParams(dimension_semantics=sem, vmem_limit_bytes=VMEM_LIMIT)


def _rms(x, g):
    return x * lax.rsqrt(jnp.mean(x * x, axis=-1, keepdims=True) + RMS_EPS) * g


def _dot(a, b):
    return jnp.dot(a, b, preferred_element_type=F32)


def _dot_nt(a, b):
    return lax.dot_general(a, b, (((1,), (1,)), ((), ())), preferred_element_type=F32)


def _dot_tn(a, b):
    return lax.dot_general(a, b, (((0,), (0,)), ((), ())), preferred_element_type=F32)


def _full(shape):
    n = len(shape)
    return pl.BlockSpec(shape, lambda *_: (0,) * n)


def _ab_in_kernel(x_ref, gpre_ref, wq2_ref, wkv_ref, wb_ref, gq_ref, gkv_ref, wuq_ref, wuk_ref, cos_ref, sin_ref,
                  ckv_ref, kpe_ref, zb_ref, qabs_ref, kvb_ref, *, period):
    h = _rms(x_ref[...], gpre_ref[...]).astype(BF16)
    zq2 = _dot(h, wq2_ref[...])
    zkv = _dot(h, wkv_ref[...])
    zb_ref[...] = _dot(h, wb_ref[...])
    ckv = _rms(zkv, gkv_ref[...])
    ckv_ref[...] = ckv
    tm = x_ref.shape[0]
    start = pl.multiple_of((pl.program_id(0) * tm) % period, 8)
    cos = cos_ref[pl.ds(start, tm), :]
    sin = sin_ref[pl.ds(start, tm), :]
    pe = zq2[:, Q_LORA:Q_LORA + ROPE_DIM]
    pe_sw = zq2[:, Q_LORA + ROPE_DIM:Q_LORA + 2 * ROPE_DIM]
    kpe = pe * cos[:, :ROPE_DIM] + pe_sw * sin[:, :ROPE_DIM]
    kpe_ref[...] = kpe
    kvb_ref[:, :KV_LORA] = ckv.astype(BF16)
    kvb_ref[:, KV_LORA:] = jnp.concatenate([kpe, jnp.zeros((kpe.shape[0], LANES - ROPE_DIM), F32)], axis=1).astype(BF16)
    qn = _rms(zq2[:, :Q_LORA], gq_ref[...]).astype(BF16)
    q2 = _dot(qn, wuq_ref[...])
    n_nope = H_A * NOPE_DIM
    for hd in range(H_A):
        qn_h = q2[:, hd * NOPE_DIM:(hd + 1) * NOPE_DIM].astype(BF16)
        qabs_ref[hd, :, 0:KV_LORA] = _dot(qn_h, wuk_ref[hd]).astype(BF16)
        r0 = n_nope + hd * LANES
        r1 = n_nope + (H_A + hd) * LANES
        qabs_ref[hd, :, KV_LORA:QK_PAD] = (q2[:, r0:r0 + LANES] * cos + q2[:, r1:r1 + LANES] * sin).astype(BF16)


def _ab_in_proj(x, g_pre, wts, cos_t, sin_t, tm):
    m = x.shape[0]
    wq2, wkv, wb, g_q, g_kv, wuq, wuk = wts
    row = lambda w: pl.BlockSpec((tm, w), lambda i: (i, 0))
    return pl.pallas_call(
        functools.partial(_ab_in_kernel, period=cos_t.shape[0] - tm),
        grid=(m // tm,),
        in_specs=[row(D_MODEL), _full(g_pre.shape), _full(wq2.shape), _full(wkv.shape), _full(wb.shape),
                  _full(g_q.shape), _full(g_kv.shape), _full(wuq.shape), _full(wuk.shape), _full(cos_t.shape),
                  _full(sin_t.shape)],
        out_specs=[row(KV_LORA), row(ROPE_DIM), row(RWKV_IN),
                   pl.BlockSpec((H_A, tm, QK_PAD), lambda i: (0, i, 0)), row(QK_PAD)],
        out_shape=[jax.ShapeDtypeStruct((m, KV_LORA), F32), jax.ShapeDtypeStruct((m, ROPE_DIM), F32),
                   jax.ShapeDtypeStruct((m, RWKV_IN), F32), jax.ShapeDtypeStruct((H_A, m, QK_PAD), BF16),
                   jax.ShapeDtypeStruct((m, QK_PAD), BF16)],
        compiler_params=_cparams(("parallel",)),
        name="ab_in_proj",
    )(x, g_pre, wq2, wkv, wb, g_q, g_kv, wuq, wuk, cos_t, sin_t)


def _attn_prompt_kernel(q_ref, kv_ref, o_ref, kpad, *, t_p, tq):
    t_pad = kpad.shape[0]

    @pl.when(pl.program_id(1) == 0)
    def _():
        kpad[:t_p, :] = kv_ref[0]
        kpad[t_p:, :] = jnp.zeros((t_pad - t_p, QK_PAD), BF16)

    n_t = t_p // tq
    tiles = [(i * tq, t_p - i * tq if i == n_t - 1 else tq) for i in range(n_t)]
    kext = [min(-(-(q0 + rows) // LANES) * LANES, t_pad) for q0, rows in tiles]
    scores = {}
    lag = ATT_SCORE_LAG
    for i in range(n_t + lag):
        if i < n_t:
            q0, rows = tiles[i]
            scores[i] = _dot_nt(q_ref[0, q0:q0 + rows, :], kpad[:kext[i], :])
        if i >= lag:
            q0, rows = tiles[i - lag]
            ke = kext[i - lag]
            s = scores.pop(i - lag)
            lo = (q0 // LANES) * LANES
            q_pos = q0 + lax.broadcasted_iota(jnp.int32, (rows, ke - lo), 0)
            k_pos = lo + lax.broadcasted_iota(jnp.int32, (rows, ke - lo), 1)
            s_diag = jnp.where(k_pos <= q_pos, s[:, lo:], -jnp.inf)
            s = jnp.concatenate([s[:, :lo], s_diag], axis=1) if lo else s_diag
            p = jnp.exp2((s - jnp.max(s, axis=-1, keepdims=True)) * (SM_SCALE * LOG2_E))
            l = jnp.sum(p, axis=-1, keepdims=True)
            o = _dot(p.astype(BF16), kpad[:ke, :KV_LORA])
            o_ref[0, q0:q0 + rows, :] = (o / l).astype(BF16)


def _attn_prompt(qabs, kvb, n_b, t_p):
    m = qabs.shape[1]
    t_pad = -(-t_p // LANES) * LANES
    return pl.pallas_call(
        functools.partial(_attn_prompt_kernel, t_p=t_p, tq=ATT_Q_TILE),
        grid=(n_b, H_A),
        in_specs=[pl.BlockSpec((1, t_p, QK_PAD), lambda b, h: (h, b, 0)),
                  pl.BlockSpec((1, t_p, QK_PAD), lambda b, h: (b, 0, 0))],
        out_specs=pl.BlockSpec((1, t_p, KV_LORA), lambda b, h: (h, b, 0)),
        out_shape=jax.ShapeDtypeStruct((H_A, m, KV_LORA), BF16),
        scratch_shapes=[pltpu.VMEM((t_pad, QK_PAD), BF16)],
        compiler_params=_cparams(("parallel", "arbitrary")),
        name="attn_prompt",
    )(qabs, kvb.reshape(n_b, t_p, QK_PAD))


def _attn_sample_kernel(pt_ref, q_ref, knew_ref, ckv_hbm, kpe_hbm, o_ref, ckv_buf, kpe_buf, sems, *, layer, n_pages,
                        t_s):
    b = pl.program_id(0)
    n_b = pl.num_programs(0)
    slot = b % 2

    def page_copies(bb, sl, p):
        page = pt_ref[bb, p]
        return (pltpu.make_async_copy(ckv_hbm.at[layer, page], ckv_buf.at[sl, p], sems.at[0, sl]),
                pltpu.make_async_copy(kpe_hbm.at[layer, page], kpe_buf.at[sl, p], sems.at[1, sl]))

    def start_all(bb, sl):
        for p in range(n_pages):
            for cp in page_copies(bb, sl, p):
                cp.start()

    @pl.when(b == 0)
    def _():
        start_all(0, 0)

    @pl.when(b + 1 < n_b)
    def _():
        start_all(b + 1, 1 - slot)

    for p in range(n_pages):
        for cp in page_copies(b, slot, p):
            cp.wait()

    q = q_ref[0]
    q_lat = q[:, :KV_LORA]
    q_pe = q[:, KV_LORA:KV_LORA + ROPE_DIM]
    rows = PAGES_PER_CHUNK * PAGE_SIZE
    n_chunks = n_pages // PAGES_PER_CHUNK

    def latent_rows(c):
        return ckv_buf[slot, c * PAGES_PER_CHUNK:(c + 1) * PAGES_PER_CHUNK].reshape(rows, KV_LORA).astype(BF16)

    parts = []
    scores = {}
    lag = ATT_SCORE_LAG
    for c in range(n_chunks + lag):
        if c < n_chunks:
            kp = jnp.concatenate([kpe_buf[slot, c * PAGES_PER_CHUNK + i] for i in range(PAGES_PER_CHUNK)], axis=1)
            scores[c] = (_dot_nt(q_lat, latent_rows(c)) + _dot(q_pe, kp.astype(BF16))) * SM_SCALE
        if c >= lag:
            s = scores.pop(c - lag)
            m_c = jnp.max(s, axis=-1, keepdims=True)
            p = jnp.exp(s - m_c)
            parts.append((m_c, jnp.sum(p, axis=-1, keepdims=True), _dot(p.astype(BF16), latent_rows(c - lag))))
    kn = knew_ref[0]
    s_new = _dot_nt(q, kn) * SM_SCALE
    step = lax.broadcasted_iota(jnp.int32, s_new.shape, 0) % t_s
    key = lax.broadcasted_iota(jnp.int32, s_new.shape, 1)
    s_new = jnp.where(key <= step, s_new, -jnp.inf)
    m_n = jnp.max(s_new, axis=-1, keepdims=True)
    p_new = jnp.exp(s_new - m_n)
    parts.append((m_n, jnp.sum(p_new, axis=-1, keepdims=True), _dot(p_new.astype(BF16), kn[:, :KV_LORA])))
    m = parts[0][0]
    for m_c, _, _ in parts[1:]:
        m = jnp.maximum(m, m_c)
    l = jnp.zeros_like(m)
    acc = jnp.zeros((q.shape[0], KV_LORA), F32)
    for m_c, l_c, acc_c in parts:
        w_c = jnp.exp(m_c - m)
        l = l + w_c * l_c
        acc = acc + w_c * acc_c
    o_ref[0] = (acc / l).astype(BF16)


def _attn_sample(page_table, q_s, k_new, cache_ckv, cache_kpe, layer, t_s):
    n_b, n_pages = page_table.shape
    n_q = q_s.shape[1]
    grid_spec = pltpu.PrefetchScalarGridSpec(
        num_scalar_prefetch=1,
        grid=(n_b,),
        in_specs=[pl.BlockSpec((1, n_q, QK_PAD), lambda b, pt: (b, 0, 0)),
                  pl.BlockSpec((1,) + k_new.shape[1:], lambda b, pt: (b, 0, 0)),
                  pl.BlockSpec(memory_space=pl.ANY),
                  pl.BlockSpec(memory_space=pl.ANY)],
        out_specs=pl.BlockSpec((1, n_q, KV_LORA), lambda b, pt: (b, 0, 0)),
        scratch_shapes=[pltpu.VMEM((2, n_pages, PAGE_SIZE, KV_LORA), F32),
                        pltpu.VMEM((2, n_pages, ROPE_DIM, PAGE_SIZE), F32),
                        pltpu.SemaphoreType.DMA((2, 2))],
    )
    return pl.pallas_call(
        functools.partial(_attn_sample_kernel, layer=layer, n_pages=n_pages, t_s=t_s),
        grid_spec=grid_spec,
        out_shape=jax.ShapeDtypeStruct((n_b, n_q, KV_LORA), BF16),
        compiler_params=_cparams(("arbitrary",)),
        name="attn_sample",
    )(page_table, q_s, k_new, cache_ckv, cache_kpe)


def _ab_out_kernel(olat_ref, ob_ref, x_ref, wuv_ref, wout_ref, gpost_ref, o_ref, cat_ref):
    n_a = H_A * V_DIM
    for hd in range(0, H_A, 2):
        pair = [_dot(olat_ref[hd + i], wuv_ref[hd + i]) for i in range(2)]
        cat_ref[:, hd * V_DIM:(hd + 2) * V_DIM] = jnp.concatenate(pair, axis=1).astype(BF16)
    cat_ref[:, n_a:] = ob_ref[...].astype(BF16)
    y = _dot(cat_ref[...], wout_ref[...])
    o_ref[...] = x_ref[...] + _rms(y, gpost_ref[...])


def _ab_out_proj(olat, o_b, x, wuv, wout, g_post, tm):
    m = x.shape[0]
    row = lambda w: pl.BlockSpec((tm, w), lambda i: (i, 0))
    return pl.pallas_call(
        _ab_out_kernel,
        grid=(m // tm,),
        in_specs=[pl.BlockSpec((H_A, tm, KV_LORA), lambda i: (0, i, 0)), row(D_B), row(D_MODEL),
                  _full(wuv.shape), _full(wout.shape), _full(g_post.shape)],
        out_specs=row(D_MODEL),
        out_shape=jax.ShapeDtypeStruct((m, D_MODEL), F32),
        scratch_shapes=[pltpu.VMEM((tm, H_A * V_DIM + D_B), BF16)],
        compiler_params=_cparams(("parallel",)),
        name="ab_out_proj",
    )(olat, o_b, x, wuv, wout, g_post)


def _ffn_kernel(xp_ref, xs_ref, gpre_ref, wup_hbm, wdown_hbm, gpost_ref, op_ref, os_ref, wup_ref, wdown_ref, st_up,
                st_dn, sems, *, ff_chunk, layer, n_p):
    @pl.when(pl.program_id(0) == 0)
    def _():
        jobs = []
        for k, (src, dst, st) in enumerate(((wup_hbm, wup_ref, st_up), (wdown_hbm, wdown_ref, st_dn))):
            rows = st.shape[1]
            for c in range(dst.shape[0] // rows):
                jobs.append((k, c, rows, src, dst, st))

        def copy(job, slot):
            k, c, rows, src, _, st = job
            return pltpu.make_async_copy(src.at[layer, c * rows:(c + 1) * rows], st.at[slot], sems.at[k, slot])

        copy(jobs[0], 0).start()
        for i, job in enumerate(jobs):
            slot = i % 2
            if i + 1 < len(jobs):
                copy(jobs[i + 1], 1 - slot).start()
            copy(job, slot).wait()
            k, c, rows, _, dst, st = job
            dst[c * rows:(c + 1) * rows, :] = st[slot].astype(BF16)

    def mlp(x_ref, o_ref):
        x = x_ref[...]
        h = _rms(x, gpre_ref[...]).astype(BF16)
        y = jnp.zeros(x.shape, F32)
        for c in range(D_FF // ff_chunk):
            u = _dot(h, wup_ref[:, c * ff_chunk:(c + 1) * ff_chunk])
            a = jnp.square(jnp.maximum(u, 0.0)).astype(BF16)
            y = y + _dot(a, wdown_ref[c * ff_chunk:(c + 1) * ff_chunk, :])
        o_ref[...] = x + _rms(y, gpost_ref[...])

    @pl.when(pl.program_id(0) < n_p)
    def _():
        mlp(xp_ref, op_ref)

    @pl.when(pl.program_id(0) == n_p)
    def _():
        mlp(xs_ref, os_ref)


def _ffn(xp, xs, g_pre, w_up, w_down, g_post, layer, tm):
    n_p = xp.shape[0] // tm
    row = pl.BlockSpec((tm, D_MODEL), lambda i: (jnp.minimum(i, n_p - 1), 0))
    return pl.pallas_call(
        functools.partial(_ffn_kernel, ff_chunk=1024, layer=layer, n_p=n_p),
        grid=(n_p + 1,),
        in_specs=[row, _full(xs.shape), _full(g_pre.shape), pl.BlockSpec(memory_space=pl.ANY),
                  pl.BlockSpec(memory_space=pl.ANY), _full(g_post.shape)],
        out_specs=[row, _full(xs.shape)],
        out_shape=[jax.ShapeDtypeStruct(xp.shape, F32), jax.ShapeDtypeStruct(xs.shape, F32)],
        scratch_shapes=[pltpu.VMEM(w_up.shape[1:], BF16), pltpu.VMEM(w_down.shape[1:], BF16),
                        pltpu.VMEM((2, FFN_STAGE_BYTES // (4 * D_FF), D_FF), F32),
                        pltpu.VMEM((2, FFN_STAGE_BYTES // (4 * D_MODEL), D_MODEL), F32),
                        pltpu.SemaphoreType.DMA((2, 2))],
        compiler_params=_cparams(("arbitrary",)),
        name="ffn",
    )(xp, xs, g_pre, w_up, w_down, g_post)


def _split_dot(x, w):
    hi = x.astype(BF16)
    lo = (x - hi.astype(F32)).astype(BF16)
    return _dot(hi, w) + _dot(lo, w)


def _rwkv_prep_kernel(zb_ref, first_ref, mu_ref, w0_ref, w2_ref, a0_ref, a2_ref, g2_ref, kk_ref, ka_ref, rk_ref,
                      ones_ref, r_o, lw_o, keff_o, v_o, kn_o, b_o, g_o, bonus_o, carry, *, shift):
    @pl.when(pl.program_id(1) == 0)
    def _():
        carry[...] = first_ref[0]

    zb = zb_ref[...]
    tm = zb.shape[0]
    if shift == 1:
        row = lax.broadcasted_iota(jnp.int32, (tm, 1), 0)
        prev = jnp.where(row == 0, carry[0:1, :], pltpu.roll(zb, 1, 0))
        carry[0:1, :] = zb[tm - 1:tm, :]
    else:
        prev = jnp.concatenate([carry[...], zb[:tm - shift]], axis=0)
        carry[...] = zb[tm - shift:]
    xm = zb + (prev - zb) * mu_ref[...]
    o3 = 3 * D_B
    o4 = o3 + DECAY_LORA
    o5 = o4 + AAA_LORA
    r = xm[:, :D_B]
    k = xm[:, D_B:2 * D_B]
    v = xm[:, 2 * D_B:o3]
    z = -(w0_ref[...] + _dot(jnp.tanh(xm[:, o3:o4]).astype(BF16), w2_ref[...].astype(BF16)))
    softplus = jnp.maximum(z, 0.0) + jnp.log(1.0 + jnp.exp(-jnp.abs(z)))
    lw_o[...] = -jnp.exp(-softplus - 0.5)
    a = 1.0 / (1.0 + jnp.exp(-(a0_ref[...] + _dot(xm[:, o4:o5].astype(BF16), a2_ref[...].astype(BF16)))))
    sg = 1.0 / (1.0 + jnp.exp(-xm[:, o5:]))
    g_o[...] = _dot(sg.astype(BF16), g2_ref[...].astype(BF16))
    ones = ones_ref[...]
    kk = k * kk_ref[...]
    kk = kk / jnp.maximum(jnp.sqrt(_split_dot(kk * kk, ones)), 1e-12)
    keff = k * (1.0 + (a - 1.0) * ka_ref[...])
    r_o[...] = r
    keff_o[...] = keff
    v_o[...] = v
    kn_o[...] = kk
    b_o[...] = kk * a
    bonus_o[...] = _split_dot(r * keff * rk_ref[...], ones) * v


def _rwkv_prep(zb, first, rw, tm, shift):
    m = zb.shape[0]
    n_seq = first.shape[0]
    n_t = m // (n_seq * tm)
    row = lambda w: pl.BlockSpec((tm, w), lambda b, j: (b * n_t + j, 0))
    return pl.pallas_call(
        functools.partial(_rwkv_prep_kernel, shift=shift),
        grid=(n_seq, n_t),
        in_specs=[row(RWKV_IN), pl.BlockSpec((1,) + first.shape[1:], lambda b, j: (b, 0, 0))]
        + [_full(w.shape) for w in rw],
        out_specs=[row(D_B)] * 8,
        out_shape=[jax.ShapeDtypeStruct((m, D_B), F32)] * 8,
        scratch_shapes=[pltpu.VMEM(first.shape[1:], F32)],
        compiler_params=_cparams(("parallel", "arbitrary")),
        name="rwkv_prep",
    )(zb, first, *rw)


def _rwkv_scan_kernel(r_ref, lw_ref, keff_ref, v_ref, kn_ref, b_ref, g_ref, bonus_ref, lnw_ref, lnb_ref, s0_ref,
                      tri_ref, o_ref, s_out_ref, s_scr, *, chunk, n_sb):
    c = pl.program_id(1)
    n_g = H_B // 4
    gw = 4 * N_B
    hw = 4 * chunk
    groups = [(j, g) for j in range(n_sb) for g in range(n_g)]

    @pl.when(c == 0)
    def _():
        for j, g in groups:
            s_scr[j, g] = jnp.concatenate([s0_ref[j, 4 * g + i] for i in range(4)], axis=1)

    def blk(idx, size):
        return ((idx >= size).astype(jnp.int32) + (idx >= 2 * size).astype(jnp.int32)
                + (idx >= 3 * size).astype(jnp.int32))

    t_i = lax.broadcasted_iota(jnp.int32, (chunk, hw), 0)
    col = lax.broadcasted_iota(jnp.int32, (chunk, hw), 1)
    s_i = col - chunk * blk(col, chunk)
    strict = s_i < t_i
    incl = s_i <= t_i
    eye = jnp.where(s_i == t_i, 1.0, 0.0).astype(F32)
    levels = []
    k_lvl = 1
    while (1 << (k_lvl - 1)) < chunk:
        same_new = (t_i >> k_lvl) == (s_i >> k_lvl)
        same_old = (t_i >> (k_lvl - 1)) == (s_i >> (k_lvl - 1))
        levels.append(strict & same_new & jnp.logical_not(same_old))
        k_lvl += 1
    bd_hh = (blk(lax.broadcasted_iota(jnp.int32, (hw, hw), 0), chunk)
             == blk(lax.broadcasted_iota(jnp.int32, (hw, hw), 1), chunk))
    bd_hg = (blk(lax.broadcasted_iota(jnp.int32, (hw, gw), 0), chunk)
             == (lax.broadcasted_iota(jnp.int32, (hw, gw), 1) >> 6))
    lane_head = lax.broadcasted_iota(jnp.int32, (N_B, gw), 1) >> 6
    bd_gg = (lax.broadcasted_iota(jnp.int32, (gw, gw), 0) >> 6) == (lax.broadcasted_iota(jnp.int32, (gw, gw), 1) >> 6)
    ones_gg = jnp.where(bd_gg, 1.0, 0.0).astype(BF16)

    def bdiag(x, mask):
        return jnp.where(mask, jnp.concatenate([x] * 4, axis=0), 0.0).astype(BF16)

    tri = tri_ref[...]
    qr, bk_d, kk_d, v_d, kq, rq, bke, tkv, g_last = {}, {}, {}, {}, {}, {}, {}, {}, {}
    for j in range(n_sb):
        lw = lw_ref[j]
        hi = lw.astype(BF16)
        r1 = lw - hi.astype(F32)
        mid = r1.astype(BF16)
        lo = (r1 - mid.astype(F32)).astype(BF16)
        cs = _dot(tri, hi) + _dot(tri, mid) + _dot(tri, lo)
        cs_last = cs[chunk - 1:chunk, :]
        g_inv = jnp.exp(-cs)
        g_end = jnp.exp(cs_last - cs)
        kq_j = kn_ref[j] * jnp.exp(cs - lw)
        rq_j = r_ref[j] * jnp.exp(cs)
        b_j = b_ref[j]
        ke_j = keff_ref[j]
        v_j = v_ref[j]
        bk_j = b_j * g_inv
        kk_j = ke_j * g_inv
        bke_j = b_j * g_end
        kke_j = ke_j * g_end
        gl_j = jnp.exp(cs_last)
        for g in range(n_g):
            sl = slice(g * gw, (g + 1) * gw)
            gr = (j, g)
            kq[gr], rq[gr], g_last[gr] = kq_j[:, sl], rq_j[:, sl], gl_j[:, sl]
            qr[gr] = jnp.concatenate([kq[gr], rq[gr]], axis=0).astype(BF16)
            bk_d[gr], kk_d[gr], v_d[gr] = bdiag(bk_j[:, sl], bd_hg), bdiag(kk_j[:, sl], bd_hg), bdiag(v_j[:, sl], bd_hg)
            bke[gr] = bke_j[:, sl]
            tkv[gr] = (v_j[:, sl], kke_j[:, sl])

    p1 = {gr: _dot_nt(qr[gr], bk_d[gr]) for gr in groups}
    p2 = {gr: _dot_nt(qr[gr], kk_d[gr]) for gr in groups}
    a_m = {gr: -p1[gr][:chunk] for gr in groups}
    t_m = {gr: eye + jnp.where(levels[0], a_m[gr], 0.0) for gr in groups}
    w_m = {gr: _dot(jnp.where(strict, p2[gr][:chunk], 0.0).astype(BF16), v_d[gr]) for gr in groups}
    for lvl in levels[1:]:
        x_m = {gr: _dot(jnp.where(lvl, a_m[gr], 0.0).astype(BF16), bdiag(t_m[gr], bd_hh)) for gr in groups}
        t_m = {gr: t_m[gr] + _dot(t_m[gr].astype(BF16), bdiag(x_m[gr], bd_hh)) for gr in groups}
    t_b = {gr: t_m[gr].astype(BF16) for gr in groups}
    kq2 = {gr: _dot(t_b[gr], bdiag(kq[gr], bd_hg)) for gr in groups}
    w2 = {gr: _dot(t_b[gr], bdiag(w_m[gr], bd_hg)) for gr in groups}
    ar = {gr: jnp.where(incl, -p1[gr][chunk:], 0.0).astype(BF16) for gr in groups}
    br = {gr: jnp.where(incl, p2[gr][chunk:], 0.0).astype(BF16) for gr in groups}
    rq2 = {gr: rq[gr] + _dot(ar[gr], bdiag(kq2[gr], bd_hg)) for gr in groups}
    y0 = {gr: _dot(ar[gr], bdiag(w2[gr], bd_hg)) + _dot(br[gr], v_d[gr]) for gr in groups}
    bke_b = {gr: bke[gr].astype(BF16) for gr in groups}
    m_full = {gr: _dot_tn(kq2[gr].astype(BF16), bke_b[gr]) for gr in groups}
    n_full = {gr: _dot_tn(jnp.concatenate([tkv[gr][0], -w2[gr]], axis=0).astype(BF16),
                          jnp.concatenate([tkv[gr][1], bke[gr]], axis=0).astype(BF16)) for gr in groups}

    s_old = {gr: s_scr[gr[0], gr[1]] for gr in groups}
    s_d = {gr: bdiag(s_old[gr], bd_gg) for gr in groups}
    y = {gr: _dot_nt(rq2[gr].astype(BF16), s_d[gr]) + y0[gr] for gr in groups}
    s_new = {}
    for gr in groups:
        n_h = jnp.zeros((N_B, gw), F32)
        for i in range(4):
            n_h = n_h + jnp.where(lane_head == i, n_full[gr][i * N_B:(i + 1) * N_B, :], 0.0)
        s_new[gr] = (s_old[gr] * g_last[gr]
                     - _dot(s_old[gr].astype(BF16), jnp.where(bd_gg, m_full[gr], 0.0).astype(BF16)) + n_h)
        s_scr[gr[0], gr[1]] = s_new[gr]

    @pl.when(c == pl.num_programs(1) - 1)
    def _():
        for j, g in groups:
            for i in range(4):
                s_out_ref[j, 4 * g + i] = s_new[(j, g)][:, i * N_B:(i + 1) * N_B]

    inv_n = 1.0 / N_B
    for j in range(n_sb):
        outs = []
        for g in range(n_g):
            y_g = y[(j, g)]
            mean = _split_dot(y_g, ones_gg) * inv_n
            d = y_g - mean
            var = _split_dot(d * d, ones_gg) * inv_n
            outs.append(d * lax.rsqrt(var + GN_EPS))
        yn = jnp.concatenate(outs, axis=-1) * lnw_ref[...] + lnb_ref[...]
        o_ref[j] = (yn + bonus_ref[j]) * g_ref[j]


def _rwkv_scan(streams, ln_w, ln_b, s0, layer, chunk, n_sb):
    n_seq, t, _ = streams[0].shape
    n_chunks = t // chunk
    tri = jnp.tril(jnp.ones((chunk, chunk), F32)).astype(BF16)
    row = pl.BlockSpec((n_sb, chunk, D_B), lambda b, c: (b, c, 0))
    st = pl.BlockSpec((n_sb, H_B, N_B, N_B), lambda b, c: (b, 0, 0, 0))
    st_in = pl.BlockSpec((None, n_sb, H_B, N_B, N_B), lambda b, c: (layer, b, 0, 0, 0))
    return pl.pallas_call(
        functools.partial(_rwkv_scan_kernel, chunk=chunk, n_sb=n_sb),
        grid=(n_seq // n_sb, n_chunks),
        in_specs=[row] * 8 + [_full(ln_w.shape), _full(ln_b.shape), st_in, _full(tri.shape)],
        out_specs=[row, st],
        out_shape=[jax.ShapeDtypeStruct((n_seq, t, D_B), F32), jax.ShapeDtypeStruct(s0.shape[1:], F32)],
        scratch_shapes=[pltpu.VMEM((n_sb, H_B // 4, N_B, 4 * N_B), F32)],
        compiler_params=_cparams(("parallel", "arbitrary")),
        name="rwkv_scan",
    )(*streams, ln_w, ln_b, s0, tri)


def _pool_prompt_kernel(x_ref, gpre_ref, wpool_ref, pscale_ref, gpost_ref, o_ref, tail_ref, hext, *, tm):
    j = pl.program_id(1)
    x = x_ref[...]
    h = _rms(x, gpre_ref[...])

    @pl.when(j == 0)
    def _():
        hext[0:W_MAX, :] = jnp.zeros((W_MAX, D_MODEL), F32)

    @pl.when(j > 0)
    def _():
        hext[0:W_MAX, :] = hext[tm:tm + W_MAX, :]

    hext[W_MAX:, :] = h
    tail_ref[0] = h[tm - W_MAX:, :]
    pos = j * tm + lax.broadcasted_iota(jnp.int32, (tm, 1), 0)
    acc = hext[...]
    ys = []
    for gi, w in enumerate(POOL_WINDOWS):
        acc = acc[:, POOL_GC * (1 if gi else 0):]
        acc = acc + pltpu.roll(acc, w // 2, 0)
        cnt = jnp.minimum(pos + 1, w).astype(F32)
        pooled = acc[W_MAX:, :POOL_GC] / cnt - h[:, gi * POOL_GC:(gi + 1) * POOL_GC]
        ys.append(_dot(pooled.astype(BF16), wpool_ref[gi]))
    y = jnp.concatenate(ys, axis=-1) * pscale_ref[...]
    o_ref[...] = x + _rms(y, gpost_ref[...])


def _pool_prompt(x, g_pre, w_pool, p_scale, g_post, n_b, t_p):
    m = x.shape[0]
    tm = POOL_TILE
    n_t = t_p // tm
    row = pl.BlockSpec((tm, D_MODEL), lambda b, j: (b * n_t + j, 0))
    return pl.pallas_call(
        functools.partial(_pool_prompt_kernel, tm=tm),
        grid=(n_b, n_t),
        in_specs=[row, _full(g_pre.shape), _full(w_pool.shape), _full(p_scale.shape), _full(g_post.shape)],
        out_specs=[row, pl.BlockSpec((1, W_MAX, D_MODEL), lambda b, j: (b, 0, 0))],
        out_shape=[jax.ShapeDtypeStruct((m, D_MODEL), F32), jax.ShapeDtypeStruct((n_b, W_MAX, D_MODEL), F32)],
        scratch_shapes=[pltpu.VMEM((W_MAX + tm, D_MODEL), F32)],
        compiler_params=_cparams(("parallel", "arbitrary")),
        name="pool_prompt",
    )(x, g_pre, w_pool, p_scale, g_post)


def _pool_sample_kernel(x_ref, pre_ref, gpre_ref, wpool_ref, pscale_ref, gpost_ref, o_ref, h_ref, *, n_b, t_s):
    x = x_ref[...]
    h = _rms(x, gpre_ref[...])
    h_ref[...] = h
    n_pre = W_MAX - 1
    rows = [pre_ref[i * n_b:(i + 1) * n_b, :] for i in range(n_pre)] + [h[t * n_b:(t + 1) * n_b, :] for t in range(t_s)]
    outs = []
    for t in range(t_s):
        ys = []
        for gi, w in enumerate(POOL_WINDOWS):
            sl = slice(gi * POOL_GC, (gi + 1) * POOL_GC)
            win = rows[n_pre + t][:, sl]
            for d in range(1, w):
                win = win + rows[n_pre + t - d][:, sl]
            pooled = win / float(w) - rows[n_pre + t][:, sl]
            ys.append(_dot(pooled.astype(BF16), wpool_ref[gi]))
        outs.append(jnp.concatenate(ys, axis=-1))
    y = jnp.concatenate(outs, axis=0) * pscale_ref[...]
    o_ref[...] = x + _rms(y, gpost_ref[...])


def _pool_sample(x, prefix, g_pre, w_pool, p_scale, g_post, n_b, t_s):
    args = (x, prefix, g_pre, w_pool, p_scale, g_post)
    return pl.pallas_call(
        functools.partial(_pool_sample_kernel, n_b=n_b, t_s=t_s),
        grid=(1,),
        in_specs=[_full(a.shape) for a in args],
        out_specs=[_full(x.shape), _full(x.shape)],
        out_shape=[jax.ShapeDtypeStruct(x.shape, F32)] * 2,
        compiler_params=_cparams(("arbitrary",)),
        name="pool_sample",
    )(*args)


def _rope_tables(pos):
    half = ROPE_DIM // 2
    inv = ROPE_BASE ** (-jnp.arange(half, dtype=F32) / half)
    ang = pos[:, None] * inv[None, :]
    cos, sin = jnp.cos(ang), jnp.sin(ang)
    reps = LANES // ROPE_DIM
    return jnp.tile(jnp.concatenate([cos, cos], -1), (1, reps)), jnp.tile(jnp.concatenate([-sin, sin], -1), (1, reps))


def _swap_halves(w):
    half = ROPE_DIM // 2
    return jnp.concatenate([w[..., half:], w[..., :half]], axis=-1)


def _ab_weights(e, w_in, g_q, w_uq, g_kv, w_uk):
    w = w_in[e]
    w_pe = w[:, Q_LORA + KV_LORA:MLA_IN]
    wq2 = jnp.concatenate([w[:, :Q_LORA], w_pe, _swap_halves(w_pe)], axis=1).astype(BF16)
    wkv = w[:, Q_LORA:Q_LORA + KV_LORA].astype(BF16)
    wb = w[:, MLA_IN:].astype(BF16)
    uq = w_uq[e].reshape(Q_LORA, H_A, NOPE_DIM + ROPE_DIM)
    pad = ((0, 0), (0, 0), (0, LANES - ROPE_DIM))
    uq_pe = uq[:, :, NOPE_DIM:]
    wuq = jnp.concatenate([uq[:, :, :NOPE_DIM].reshape(Q_LORA, H_A * NOPE_DIM),
                           jnp.pad(uq_pe, pad).reshape(Q_LORA, H_A * LANES),
                           jnp.pad(_swap_halves(uq_pe), pad).reshape(Q_LORA, H_A * LANES)], axis=1).astype(BF16)
    wuk = jnp.transpose(w_uk[e], (1, 2, 0)).astype(BF16)
    return wq2, wkv, wb, g_q[e][None], g_kv[e][None], wuq, wuk


def _rwkv_weights(e, mu_shift, w0, w2, a0, a2, g2, k_k, k_a, r_k):
    head = jnp.arange(D_B) // N_B
    ones = (head[:, None] == head[None, :]).astype(BF16)
    return (mu_shift[e][None], w0[e][None], w2[e], a0[e][None], a2[e], g2[e],
            k_k[e][None], k_a[e][None], r_k[e].reshape(1, D_B), ones)


def kernel(x_prompt, x_sample, cache_ckv, cache_kpe, page_table, state_wkv, state_shift, state_pool, meta_tokens,
           g_mix_pre, g_mix_post, g_ffn_pre, g_ffn_post, w_in, g_q, w_uq, g_kv, w_uk, w_uv, mu_shift, w0, w2, a0, a2,
           g2, k_k, k_a, r_k, ln_w, ln_b, w_out, w_pool, pool_scale, w_up, w_down):
    n_bp, seq, _ = x_prompt.shape
    n_bs, t_s, _ = x_sample.shape
    depth = g_mix_pre.shape[0]
    t_p = seq + N_META
    n_pages = page_table.shape[1]
    past = n_pages * PAGE_SIZE
    m_p = n_bp * t_p
    m_s = n_bs * t_s

    meta = jnp.broadcast_to(meta_tokens[None].astype(x_prompt.dtype), (n_bp, N_META, D_MODEL))
    xp = jnp.concatenate([meta, x_prompt], axis=1).reshape(m_p, D_MODEL)
    xs = jnp.transpose(x_sample, (1, 0, 2)).reshape(m_s, D_MODEL)

    tm_s = m_s
    cos_p, sin_p = _rope_tables((jnp.arange(t_p + ROW_TILE) % t_p).astype(F32))
    cos_s, sin_s = _rope_tables((past + (jnp.arange(m_s + tm_s) % m_s) // n_bs).astype(F32))
    cache_kpe_t = jnp.swapaxes(cache_kpe, 2, 3)

    ckv_p, kpe_p, wkv_p, shift_p, pool_p = [], [], [], [], []
    ckv_s, kpe_s, wkv_s, shift_s, pool_s = [], [], [], [], []
    for l in range(depth):
        g_pre, g_post = g_mix_pre[l][None], g_mix_post[l][None]
        if l % 2 == 0:
            e = l // 2
            ab_w = _ab_weights(e, w_in, g_q, w_uq, g_kv, w_uk)
            rw = _rwkv_weights(e, mu_shift, w0, w2, a0, a2, g2, k_k, k_a, r_k)
            wuv = jnp.transpose(w_uv[e], (1, 0, 2)).astype(BF16)
            wout = w_out[e].astype(BF16)
            lnw, lnb = ln_w[e][None], ln_b[e][None]

            ckv, kpe, zb, qabs, kvb = _ab_in_proj(xp, g_pre, ab_w, cos_p, sin_p, ROW_TILE)
            olat = _attn_prompt(qabs, kvb, n_bp, t_p)
            zb3 = zb.reshape(n_bp, t_p, RWKV_IN)
            first = jnp.zeros((n_bp, 8, RWKV_IN), F32)
            streams = [a.reshape(n_bp, t_p, D_B) for a in _rwkv_prep(zb, first, rw, PREP_TILE, 1)]
            o_b, s_fin = _rwkv_scan(streams, lnw, lnb, jnp.zeros((1, n_bp, H_B, N_B, N_B), F32), 0, SCAN_CHUNK_P,
                                    SCAN_SEQS_P)
            xp = _ab_out_proj(olat, o_b.reshape(m_p, D_B), xp, wuv, wout, g_post, ROW_TILE)
            ckv_p.append(ckv.reshape(n_bp, t_p, KV_LORA))
            kpe_p.append(kpe.reshape(n_bp, t_p, ROPE_DIM))
            wkv_p.append(s_fin)
            shift_p.append(zb3[:, -1])

            ckv, kpe, zb, qabs, kvb = _ab_in_proj(xs, g_pre, ab_w, cos_s, sin_s, tm_s)
            q_s = jnp.transpose(qabs.reshape(H_A, t_s, n_bs, QK_PAD), (2, 0, 1, 3)).reshape(n_bs, H_A * t_s, QK_PAD)
            k_new = jnp.pad(jnp.transpose(kvb.reshape(t_s, n_bs, QK_PAD), (1, 0, 2)),
                            ((0, 0), (0, NEW_KEY_ROWS - t_s), (0, 0)))
            o_s = _attn_sample(page_table, q_s, k_new, cache_ckv, cache_kpe_t, e, t_s)
            olat = jnp.transpose(o_s.reshape(n_bs, H_A, t_s, KV_LORA), (1, 2, 0, 3)).reshape(H_A, m_s, KV_LORA)
            streams = _rwkv_prep(zb, state_shift[e].astype(F32)[None], rw, tm_s, n_bs)
            pad_t = lambda a: jnp.pad(jnp.transpose(a.reshape(t_s, n_bs, D_B), (1, 0, 2)),
                                      ((0, 0), (0, SCAN_CHUNK_S - t_s), (0, 0)))
            o_b, s_fin = _rwkv_scan([pad_t(a) for a in streams], lnw, lnb, state_wkv[e].astype(F32)[None], 0,
                                    SCAN_CHUNK_S, SCAN_SEQS_S)
            o_b = jnp.transpose(o_b[:, :t_s], (1, 0, 2)).reshape(m_s, D_B)
            xs = _ab_out_proj(olat, o_b, xs, wuv, wout, g_post, tm_s)
            ckv_s.append(jnp.transpose(ckv.reshape(t_s, n_bs, KV_LORA), (1, 0, 2)))
            kpe_s.append(jnp.transpose(kpe.reshape(t_s, n_bs, ROPE_DIM), (1, 0, 2)))
            wkv_s.append(s_fin)
            shift_s.append(zb[m_s - n_bs:])
        else:
            o = l // 2
            wp = w_pool[o].astype(BF16)
            ps = pool_scale[o][None]
            xp, tail = _pool_prompt(xp, g_pre, wp, ps, g_post, n_bp, t_p)
            pool_p.append(tail[:, 1:])
            prefix = jnp.transpose(state_pool[o].astype(F32), (1, 0, 2))
            xs, h_s = _pool_sample(xs, prefix.reshape((W_MAX - 1) * n_bs, D_MODEL), g_pre, wp, ps, g_post, n_bs, t_s)
            full = jnp.concatenate([prefix, h_s.reshape(t_s, n_bs, D_MODEL)], axis=0)
            pool_s.append(jnp.transpose(full[-(W_MAX - 1):], (1, 0, 2)))
        gfp, gfo = g_ffn_pre[l][None], g_ffn_post[l][None]
        xp, xs = _ffn(xp, xs, gfp, w_up, w_down, gfo, l, ROW_TILE)

    y_prompt = xp.reshape(n_bp, t_p, D_MODEL)[:, N_META:]
    y_sample = jnp.transpose(xs.reshape(t_s, n_bs, D_MODEL), (1, 0, 2))
    return (y_prompt, y_sample, jnp.stack(ckv_p), jnp.stack(kpe_p), jnp.stack(wkv_p), jnp.stack(shift_p),
            jnp.stack(pool_p), jnp.stack(ckv_s), jnp.stack(kpe_s), jnp.stack(wkv_s), jnp.stack(shift_s),
            jnp.stack(pool_s))
```

```python
import functools

import jax
import jax.numpy as jnp
from jax import lax
from jax.experimental import pallas as pl
from jax.experimental.pallas import tpu as pltpu

F32 = jnp.float32
BF16 = jnp.bfloat16

D_MODEL = 1024
N_META = 16
PAGE_SIZE = 128
H_A = 8
Q_LORA = 384
KV_LORA = 256
NOPE_DIM = 64
ROPE_DIM = 32
V_DIM = 64
ROPE_BASE = 10000.0
SM_SCALE = (NOPE_DIM + ROPE_DIM) ** -0.5
LOG2_E = 1.4426950408889634
H_B = 8
N_B = 64
D_B = H_B * N_B
DECAY_LORA = 64
AAA_LORA = 64
GATE_LORA = 160
RWKV_IN = 3 * D_B + DECAY_LORA + AAA_LORA + GATE_LORA
MLA_IN = Q_LORA + KV_LORA + ROPE_DIM
GN_EPS = 64e-5
POOL_WINDOWS = (2, 4, 8, 16)
POOL_GC = D_MODEL // len(POOL_WINDOWS)
W_MAX = 16
D_FF = 4 * D_MODEL
RMS_EPS = 1e-6

LANES = 128
QK_PAD = KV_LORA + LANES
VMEM_LIMIT = 56 * 1024 * 1024
ROW_TILE = 384
ATT_Q_TILE = 256
ATT_SCORE_LAG = 2
POOL_TILE = 688
PREP_TILE = 344
SCAN_CHUNK_P = 48
SCAN_CHUNK_S = 8
SCAN_SEQS_P = 4
SCAN_SEQS_S = 8
NEW_KEY_ROWS = 16
N_STREAMS = 8
PAGES_PER_CHUNK = 8
FFN_STAGE_BYTES = 2 * 1024 * 1024


def _cparams(sem):
    return pltpu.CompilerParams(dimension_semantics=sem, vmem_limit_bytes=VMEM_LIMIT)


def _rms(x, g):
    return x * lax.rsqrt(jnp.mean(x * x, axis=-1, keepdims=True) + RMS_EPS) * g


def _dot(a, b):
    return jnp.dot(a, b, preferred_element_type=F32)


def _dot_nt(a, b):
    return lax.dot_general(a, b, (((1,), (1,)), ((), ())), preferred_element_type=F32)


def _dot_tn(a, b):
    return lax.dot_general(a, b, (((0,), (0,)), ((), ())), preferred_element_type=F32)


def _full(shape):
    n = len(shape)
    return pl.BlockSpec(shape, lambda *_: (0,) * n)


def _ab_in_kernel(x_ref, gpre_ref, wq2_ref, wkv_ref, wb_ref, gq_ref, gkv_ref, wuq_ref, wuk_ref, spread_ref, cos_ref, sin_ref,
                  ckv_ref, kpe_ref, zb_ref, qabs_ref, kvb_ref, *, period):
    h = _rms(x_ref[...], gpre_ref[...]).astype(BF16)
    zq2 = _dot(h, wq2_ref[...])
    zkv = _dot(h, wkv_ref[...])
    zb_ref[...] = _dot(h, wb_ref[...])
    ckv = _rms(zkv, gkv_ref[...])
    ckv_ref[...] = ckv
    tm = x_ref.shape[0]
    start = pl.multiple_of((pl.program_id(0) * tm) % period, 8)
    cos = cos_ref[pl.ds(start, tm), :]
    sin = sin_ref[pl.ds(start, tm), :]
    pe = zq2[:, Q_LORA:Q_LORA + ROPE_DIM]
    pe_sw = zq2[:, Q_LORA + ROPE_DIM:Q_LORA + 2 * ROPE_DIM]
    kpe = pe * cos[:, :ROPE_DIM] + pe_sw * sin[:, :ROPE_DIM]
    kpe_ref[...] = kpe
    kvb_ref[:, :KV_LORA] = ckv.astype(BF16)
    kvb_ref[:, KV_LORA:] = jnp.concatenate([kpe, jnp.zeros((kpe.shape[0], LANES - ROPE_DIM), F32)], axis=1).astype(BF16)
    qn = _rms(zq2[:, :Q_LORA], gq_ref[...]).astype(BF16)
    q2 = _dot(qn, wuq_ref[...])
    n_nope = H_A * NOPE_DIM
    n_pe = H_A * ROPE_DIM
    reps = n_pe // LANES
    q_pe = (q2[:, n_nope:n_nope + n_pe] * jnp.concatenate([cos] * reps, axis=1)
            + q2[:, n_nope + n_pe:] * jnp.concatenate([sin] * reps, axis=1)).astype(BF16)
    slabs = _dot(q_pe, spread_ref[...]).astype(BF16)
    for hd in range(H_A):
        qn_h = q2[:, hd * NOPE_DIM:(hd + 1) * NOPE_DIM].astype(BF16)
        qabs_ref[hd, :, 0:KV_LORA] = _dot(qn_h, wuk_ref[hd]).astype(BF16)
        qabs_ref[hd, :, KV_LORA:QK_PAD] = slabs[:, hd * LANES:(hd + 1) * LANES]


def _ab_in_proj(x, g_pre, wts, cos_t, sin_t, tm):
    m = x.shape[0]
    wq2, wkv, wb, g_q, g_kv, wuq, wuk, spread = wts
    row = lambda w: pl.BlockSpec((tm, w), lambda i: (i, 0))
    return pl.pallas_call(
        functools.partial(_ab_in_kernel, period=cos_t.shape[0] - tm),
        grid=(m // tm,),
        in_specs=[row(D_MODEL), _full(g_pre.shape), _full(wq2.shape), _full(wkv.shape), _full(wb.shape),
                  _full(g_q.shape), _full(g_kv.shape), _full(wuq.shape), _full(wuk.shape), _full(spread.shape),
                  _full(cos_t.shape),
                  _full(sin_t.shape)],
        out_specs=[row(KV_LORA), row(ROPE_DIM), row(RWKV_IN),
                   pl.BlockSpec((H_A, tm, QK_PAD), lambda i: (0, i, 0)), row(QK_PAD)],
        out_shape=[jax.ShapeDtypeStruct((m, KV_LORA), F32), jax.ShapeDtypeStruct((m, ROPE_DIM), F32),
                   jax.ShapeDtypeStruct((m, RWKV_IN), F32), jax.ShapeDtypeStruct((H_A, m, QK_PAD), BF16),
                   jax.ShapeDtypeStruct((m, QK_PAD), BF16)],
        compiler_params=_cparams(("parallel",)),
        name="ab_in_proj",
    )(x, g_pre, wq2, wkv, wb, g_q, g_kv, wuq, wuk, spread, cos_t, sin_t)


def _attn_prompt_kernel(q_ref, kv_ref, o_ref, kpad, *, t_p, tq):
    t_pad = kpad.shape[0]

    @pl.when(pl.program_id(1) == 0)
    def _():
        kpad[:t_p, :] = kv_ref[0]
        kpad[t_p:, :] = jnp.zeros((t_pad - t_p, QK_PAD), BF16)

    n_t = t_p // tq
    tiles = [(i * tq, t_p - i * tq if i == n_t - 1 else tq) for i in range(n_t)]
    kext = [min(-(-(q0 + rows) // LANES) * LANES, t_pad) for q0, rows in tiles]
    scores = {}
    lag = ATT_SCORE_LAG
    for i in range(n_t + lag):
        if i < n_t:
            q0, rows = tiles[i]
            scores[i] = _dot_nt(q_ref[0, q0:q0 + rows, :], kpad[:kext[i], :])
        if i >= lag:
            q0, rows = tiles[i - lag]
            ke = kext[i - lag]
            s = scores.pop(i - lag)
            lo = (q0 // LANES) * LANES
            q_pos = q0 + lax.broadcasted_iota(jnp.int32, (rows, ke - lo), 0)
            k_pos = lo + lax.broadcasted_iota(jnp.int32, (rows, ke - lo), 1)
            s_diag = jnp.where(k_pos <= q_pos, s[:, lo:], -jnp.inf)
            s = jnp.concatenate([s[:, :lo], s_diag], axis=1) if lo else s_diag
            p = jnp.exp2((s - jnp.max(s, axis=-1, keepdims=True)) * (SM_SCALE * LOG2_E))
            l = jnp.sum(p, axis=-1, keepdims=True)
            o = _dot(p.astype(BF16), kpad[:ke, :KV_LORA])
            o_ref[0, q0:q0 + rows, :] = (o / l).astype(BF16)


def _attn_prompt(qabs, kvb, n_b, t_p):
    m = qabs.shape[1]
    t_pad = -(-t_p // LANES) * LANES
    return pl.pallas_call(
        functools.partial(_attn_prompt_kernel, t_p=t_p, tq=ATT_Q_TILE),
        grid=(n_b, H_A),
        in_specs=[pl.BlockSpec((1, t_p, QK_PAD), lambda b, h: (h, b, 0)),
                  pl.BlockSpec((1, t_p, QK_PAD), lambda b, h: (b, 0, 0))],
        out_specs=pl.BlockSpec((1, t_p, KV_LORA), lambda b, h: (h, b, 0)),
        out_shape=jax.ShapeDtypeStruct((H_A, m, KV_LORA), BF16),
        scratch_shapes=[pltpu.VMEM((t_pad, QK_PAD), BF16)],
        compiler_params=_cparams(("parallel", "arbitrary")),
        name="attn_prompt",
    )(qabs, kvb.reshape(n_b, t_p, QK_PAD))


def _attn_sample_kernel(pt_ref, q_ref, knew_ref, ckv_hbm, kpe_hbm, o_ref, ckv_buf, kpe_buf, sems, *, layer, n_pages,
                        t_s):
    b = pl.program_id(0)
    n_b = pl.num_programs(0)
    slot = b % 2

    def page_copies(bb, sl, p):
        page = pt_ref[bb, p]
        return (pltpu.make_async_copy(ckv_hbm.at[layer, page], ckv_buf.at[sl, p], sems.at[0, sl]),
                pltpu.make_async_copy(kpe_hbm.at[layer, page], kpe_buf.at[sl, p], sems.at[1, sl]))

    def start_all(bb, sl):
        for p in range(n_pages):
            for cp in page_copies(bb, sl, p):
                cp.start()

    @pl.when(b == 0)
    def _():
        start_all(0, 0)

    @pl.when(b + 1 < n_b)
    def _():
        start_all(b + 1, 1 - slot)

    for p in range(n_pages):
        for cp in page_copies(b, slot, p):
            cp.wait()

    q = q_ref[0]
    q_lat = q[:, :KV_LORA]
    q_pe = q[:, KV_LORA:KV_LORA + ROPE_DIM]
    rows = PAGES_PER_CHUNK * PAGE_SIZE
    n_chunks = n_pages // PAGES_PER_CHUNK

    def latent_rows(c):
        return ckv_buf[slot, c * PAGES_PER_CHUNK:(c + 1) * PAGES_PER_CHUNK].reshape(rows, KV_LORA).astype(BF16)

    parts = []
    scores = {}
    lag = ATT_SCORE_LAG
    for c in range(n_chunks + lag):
        if c < n_chunks:
            kp = jnp.concatenate([kpe_buf[slot, c * PAGES_PER_CHUNK + i] for i in range(PAGES_PER_CHUNK)], axis=1)
            scores[c] = (_dot_nt(q_lat, latent_rows(c)) + _dot(q_pe, kp.astype(BF16))) * SM_SCALE
        if c >= lag:
            s = scores.pop(c - lag)
            m_c = jnp.max(s, axis=-1, keepdims=True)
            p = jnp.exp(s - m_c)
            parts.append((m_c, jnp.sum(p, axis=-1, keepdims=True), _dot(p.astype(BF16), latent_rows(c - lag))))
    kn = knew_ref[0]
    s_new = _dot_nt(q, kn) * SM_SCALE
    step = lax.broadcasted_iota(jnp.int32, s_new.shape, 0) % t_s
    key = lax.broadcasted_iota(jnp.int32, s_new.shape, 1)
    s_new = jnp.where(key <= step, s_new, -jnp.inf)
    m_n = jnp.max(s_new, axis=-1, keepdims=True)
    p_new = jnp.exp(s_new - m_n)
    parts.append((m_n, jnp.sum(p_new, axis=-1, keepdims=True), _dot(p_new.astype(BF16), kn[:, :KV_LORA])))
    m = parts[0][0]
    for m_c, _, _ in parts[1:]:
        m = jnp.maximum(m, m_c)
    l = jnp.zeros_like(m)
    acc = jnp.zeros((q.shape[0], KV_LORA), F32)
    for m_c, l_c, acc_c in parts:
        w_c = jnp.exp(m_c - m)
        l = l + w_c * l_c
        acc = acc + w_c * acc_c
    o_ref[0] = (acc / l).astype(BF16)


def _attn_sample(page_table, q_s, k_new, cache_ckv, cache_kpe, layer, t_s):
    n_b, n_pages = page_table.shape
    n_q = q_s.shape[1]
    grid_spec = pltpu.PrefetchScalarGridSpec(
        num_scalar_prefetch=1,
        grid=(n_b,),
        in_specs=[pl.BlockSpec((1, n_q, QK_PAD), lambda b, pt: (b, 0, 0)),
                  pl.BlockSpec((1,) + k_new.shape[1:], lambda b, pt: (b, 0, 0)),
                  pl.BlockSpec(memory_space=pl.ANY),
                  pl.BlockSpec(memory_space=pl.ANY)],
        out_specs=pl.BlockSpec((1, n_q, KV_LORA), lambda b, pt: (b, 0, 0)),
        scratch_shapes=[pltpu.VMEM((2, n_pages, PAGE_SIZE, KV_LORA), F32),
                        pltpu.VMEM((2, n_pages, ROPE_DIM, PAGE_SIZE), F32),
                        pltpu.SemaphoreType.DMA((2, 2))],
    )
    return pl.pallas_call(
        functools.partial(_attn_sample_kernel, layer=layer, n_pages=n_pages, t_s=t_s),
        grid_spec=grid_spec,
        out_shape=jax.ShapeDtypeStruct((n_b, n_q, KV_LORA), BF16),
        compiler_params=_cparams(("arbitrary",)),
        name="attn_sample",
    )(page_table, q_s, k_new, cache_ckv, cache_kpe)


def _ab_out_kernel(olat_ref, ob_ref, x_ref, wuv_ref, wout_ref, gpost_ref, o_ref, cat_ref):
    n_a = H_A * V_DIM
    for hd in range(0, H_A, 2):
        pair = [_dot(olat_ref[hd + i], wuv_ref[hd + i]) for i in range(2)]
        cat_ref[:, hd * V_DIM:(hd + 2) * V_DIM] = jnp.concatenate(pair, axis=1).astype(BF16)
    cat_ref[:, n_a:] = ob_ref[...].astype(BF16)
    y = _dot(cat_ref[...], wout_ref[...])
    o_ref[...] = x_ref[...] + _rms(y, gpost_ref[...])


def _ab_out_proj(olat, o_b, x, wuv, wout, g_post, tm):
    m = x.shape[0]
    row = lambda w: pl.BlockSpec((tm, w), lambda i: (i, 0))
    return pl.pallas_call(
        _ab_out_kernel,
        grid=(m // tm,),
        in_specs=[pl.BlockSpec((H_A, tm, KV_LORA), lambda i: (0, i, 0)), row(D_B), row(D_MODEL),
                  _full(wuv.shape), _full(wout.shape), _full(g_post.shape)],
        out_specs=row(D_MODEL),
        out_shape=jax.ShapeDtypeStruct((m, D_MODEL), F32),
        scratch_shapes=[pltpu.VMEM((tm, H_A * V_DIM + D_B), BF16)],
        compiler_params=_cparams(("parallel",)),
        name="ab_out_proj",
    )(olat, o_b, x, wuv, wout, g_post)


def _ffn_kernel(xp_ref, xs_ref, gpre_ref, wup_hbm, wdown_hbm, gpost_ref, op_ref, os_ref, wup_ref, wdown_ref, st_up,
                st_dn, sems, *, ff_chunk, layer, n_p):
    @pl.when(pl.program_id(0) == 0)
    def _():
        jobs = []
        for k, (src, dst, st) in enumerate(((wup_hbm, wup_ref, st_up), (wdown_hbm, wdown_ref, st_dn))):
            rows = st.shape[1]
            for c in range(dst.shape[0] // rows):
                jobs.append((k, c, rows, src, dst, st))

        def copy(job, slot):
            k, c, rows, src, _, st = job
            return pltpu.make_async_copy(src.at[layer, c * rows:(c + 1) * rows], st.at[slot], sems.at[k, slot])

        copy(jobs[0], 0).start()
        for i, job in enumerate(jobs):
            slot = i % 2
            if i + 1 < len(jobs):
                copy(jobs[i + 1], 1 - slot).start()
            copy(job, slot).wait()
            k, c, rows, _, dst, st = job
            dst[c * rows:(c + 1) * rows, :] = st[slot].astype(BF16)

    def mlp(x_ref, o_ref):
        x = x_ref[...]
        h = _rms(x, gpre_ref[...]).astype(BF16)
        y = jnp.zeros(x.shape, F32)
        for c in range(D_FF // ff_chunk):
            u = _dot(h, wup_ref[:, c * ff_chunk:(c + 1) * ff_chunk])
            a = jnp.square(jnp.maximum(u, 0.0)).astype(BF16)
            y = y + _dot(a, wdown_ref[c * ff_chunk:(c + 1) * ff_chunk, :])
        o_ref[...] = x + _rms(y, gpost_ref[...])

    @pl.when(pl.program_id(0) < n_p)
    def _():
        mlp(xp_ref, op_ref)

    @pl.when(pl.program_id(0) == n_p)
    def _():
        mlp(xs_ref, os_ref)


def _ffn(xp, xs, g_pre, w_up, w_down, g_post, layer, tm):
    n_p = xp.shape[0] // tm
    row = pl.BlockSpec((tm, D_MODEL), lambda i: (jnp.minimum(i, n_p - 1), 0))
    return pl.pallas_call(
        functools.partial(_ffn_kernel, ff_chunk=1024, layer=layer, n_p=n_p),
        grid=(n_p + 1,),
        in_specs=[row, _full(xs.shape), _full(g_pre.shape), pl.BlockSpec(memory_space=pl.ANY),
                  pl.BlockSpec(memory_space=pl.ANY), _full(g_post.shape)],
        out_specs=[row, _full(xs.shape)],
        out_shape=[jax.ShapeDtypeStruct(xp.shape, F32), jax.ShapeDtypeStruct(xs.shape, F32)],
        scratch_shapes=[pltpu.VMEM(w_up.shape[1:], BF16), pltpu.VMEM(w_down.shape[1:], BF16),
                        pltpu.VMEM((2, FFN_STAGE_BYTES // (4 * D_FF), D_FF), F32),
                        pltpu.VMEM((2, FFN_STAGE_BYTES // (4 * D_MODEL), D_MODEL), F32),
                        pltpu.SemaphoreType.DMA((2, 2))],
        compiler_params=_cparams(("arbitrary",)),
        name="ffn",
    )(xp, xs, g_pre, w_up, w_down, g_post)


def _split_dot(x, w):
    hi = x.astype(BF16)
    lo = (x - hi.astype(F32)).astype(BF16)
    return _dot(hi, w) + _dot(lo, w)


def _rwkv_prep_kernel(zb_ref, first_ref, mu_ref, w0_ref, w2_ref, a0_ref, a2_ref, g2_ref, kk_ref, ka_ref, rk_ref,
                      ones_ref, out_ref, carry, *, shift):
    @pl.when(pl.program_id(1) == 0)
    def _():
        carry[...] = first_ref[0]

    zb = zb_ref[...]
    tm = zb.shape[0]
    if shift == 1:
        row = lax.broadcasted_iota(jnp.int32, (tm, 1), 0)
        prev = jnp.where(row == 0, carry[0:1, :], pltpu.roll(zb, 1, 0))
        carry[0:1, :] = zb[tm - 1:tm, :]
    else:
        prev = jnp.concatenate([carry[...], zb[:tm - shift]], axis=0)
        carry[...] = zb[tm - shift:]
    xm = zb + (prev - zb) * mu_ref[...]
    o3 = 3 * D_B
    o4 = o3 + DECAY_LORA
    o5 = o4 + AAA_LORA
    r = xm[:, :D_B]
    k = xm[:, D_B:2 * D_B]
    v = xm[:, 2 * D_B:o3]
    z = -(w0_ref[...] + _dot(jnp.tanh(xm[:, o3:o4]).astype(BF16), w2_ref[...].astype(BF16)))
    softplus = jnp.maximum(z, 0.0) + jnp.log(1.0 + jnp.exp(-jnp.abs(z)))
    out_ref[1] = -jnp.exp(-softplus - 0.5)
    a = 1.0 / (1.0 + jnp.exp(-(a0_ref[...] + _dot(xm[:, o4:o5].astype(BF16), a2_ref[...].astype(BF16)))))
    sg = 1.0 / (1.0 + jnp.exp(-xm[:, o5:]))
    out_ref[6] = _dot(sg.astype(BF16), g2_ref[...].astype(BF16))
    ones = ones_ref[...]
    kk = k * kk_ref[...]
    kk = kk / jnp.maximum(jnp.sqrt(_split_dot(kk * kk, ones)), 1e-12)
    keff = k * (1.0 + (a - 1.0) * ka_ref[...])
    out_ref[0] = r
    out_ref[2] = keff
    out_ref[3] = v
    out_ref[4] = kk
    out_ref[5] = kk * a
    out_ref[7] = _split_dot(r * keff * rk_ref[...], ones) * v


def _rwkv_prep(zb, first, rw, tm, shift):
    m = zb.shape[0]
    n_seq = first.shape[0]
    n_t = m // (n_seq * tm)
    row = lambda w: pl.BlockSpec((tm, w), lambda b, j: (b * n_t + j, 0))
    return pl.pallas_call(
        functools.partial(_rwkv_prep_kernel, shift=shift),
        grid=(n_seq, n_t),
        in_specs=[row(RWKV_IN), pl.BlockSpec((1,) + first.shape[1:], lambda b, j: (b, 0, 0))]
        + [_full(w.shape) for w in rw],
        out_specs=pl.BlockSpec((N_STREAMS, tm, D_B), lambda b, j: (0, b * n_t + j, 0)),
        out_shape=jax.ShapeDtypeStruct((N_STREAMS, m, D_B), F32),
        scratch_shapes=[pltpu.VMEM(first.shape[1:], F32)],
        compiler_params=_cparams(("parallel", "arbitrary")),
        name="rwkv_prep",
    )(zb, first, *rw)


def _rwkv_scan_kernel(r_ref, lw_ref, keff_ref, v_ref, kn_ref, b_ref, g_ref, bonus_ref, lnw_ref, lnb_ref, s0_ref,
                      tri_ref, o_ref, s_out_ref, s_scr, *, chunk, n_sb):
    c = pl.program_id(1)
    n_g = H_B // 4
    gw = 4 * N_B
    hw = 4 * chunk
    groups = [(j, g) for j in range(n_sb) for g in range(n_g)]

    @pl.when(c == 0)
    def _():
        for j, g in groups:
            s_scr[j, g] = jnp.concatenate([s0_ref[j, 4 * g + i] for i in range(4)], axis=1)

    def blk(idx, size):
        return ((idx >= size).astype(jnp.int32) + (idx >= 2 * size).astype(jnp.int32)
                + (idx >= 3 * size).astype(jnp.int32))

    t_i = lax.broadcasted_iota(jnp.int32, (chunk, hw), 0)
    col = lax.broadcasted_iota(jnp.int32, (chunk, hw), 1)
    s_i = col - chunk * blk(col, chunk)
    strict = s_i < t_i
    incl = s_i <= t_i
    eye = jnp.where(s_i == t_i, 1.0, 0.0).astype(F32)
    levels = []
    k_lvl = 1
    while (1 << (k_lvl - 1)) < chunk:
        same_new = (t_i >> k_lvl) == (s_i >> k_lvl)
        same_old = (t_i >> (k_lvl - 1)) == (s_i >> (k_lvl - 1))
        levels.append(strict & same_new & jnp.logical_not(same_old))
        k_lvl += 1
    bd_hh = (blk(lax.broadcasted_iota(jnp.int32, (hw, hw), 0), chunk)
             == blk(lax.broadcasted_iota(jnp.int32, (hw, hw), 1), chunk))
    bd_hg = (blk(lax.broadcasted_iota(jnp.int32, (hw, gw), 0), chunk)
             == (lax.broadcasted_iota(jnp.int32, (hw, gw), 1) >> 6))
    lane_head = lax.broadcasted_iota(jnp.int32, (N_B, gw), 1) >> 6
    bd_gg = (lax.broadcasted_iota(jnp.int32, (gw, gw), 0) >> 6) == (lax.broadcasted_iota(jnp.int32, (gw, gw), 1) >> 6)
    ones_gg = jnp.where(bd_gg, 1.0, 0.0).astype(BF16)

    def bdiag(x, mask):
        return jnp.where(mask, jnp.concatenate([x] * 4, axis=0), 0.0).astype(BF16)

    tri = tri_ref[...]
    qr, bk_d, kk_d, v_d, kq, rq, bke, tkv, g_last = {}, {}, {}, {}, {}, {}, {}, {}, {}
    for j in range(n_sb):
        lw = lw_ref[j]
        hi = lw.astype(BF16)
        r1 = lw - hi.astype(F32)
        mid = r1.astype(BF16)
        lo = (r1 - mid.astype(F32)).astype(BF16)
        cs = _dot(tri, hi) + _dot(tri, mid) + _dot(tri, lo)
        cs_last = cs[chunk - 1:chunk, :]
        g_inv = jnp.exp(-cs)
        g_end = jnp.exp(cs_last - cs)
        kq_j = kn_ref[j] * jnp.exp(cs - lw)
        rq_j = r_ref[j] * jnp.exp(cs)
        b_j = b_ref[j]
        ke_j = keff_ref[j]
        v_j = v_ref[j]
        bk_j = b_j * g_inv
        kk_j = ke_j * g_inv
        bke_j = b_j * g_end
        kke_j = ke_j * g_end
        gl_j = jnp.exp(cs_last)
        for g in range(n_g):
            sl = slice(g * gw, (g + 1) * gw)
            gr = (j, g)
            kq[gr], rq[gr], g_last[gr] = kq_j[:, sl], rq_j[:, sl], gl_j[:, sl]
            qr[gr] = jnp.concatenate([kq[gr], rq[gr]], axis=0).astype(BF16)
            bk_d[gr], kk_d[gr], v_d[gr] = bdiag(bk_j[:, sl], bd_hg), bdiag(kk_j[:, sl], bd_hg), bdiag(v_j[:, sl], bd_hg)
            bke[gr] = bke_j[:, sl]
            tkv[gr] = (v_j[:, sl], kke_j[:, sl])

    p1 = {gr: _dot_nt(qr[gr], bk_d[gr]) for gr in groups}
    p2 = {gr: _dot_nt(qr[gr], kk_d[gr]) for gr in groups}
    a_m = {gr: -p1[gr][:chunk] for gr in groups}
    t_m = {gr: eye + jnp.where(levels[0], a_m[gr], 0.0) for gr in groups}
    w_m = {gr: _dot(jnp.where(strict, p2[gr][:chunk], 0.0).astype(BF16), v_d[gr]) for gr in groups}
    for lvl in levels[1:]:
        x_m = {gr: _dot(jnp.where(lvl, a_m[gr], 0.0).astype(BF16), bdiag(t_m[gr], bd_hh)) for gr in groups}
        t_m = {gr: t_m[gr] + _dot(t_m[gr].astype(BF16), bdiag(x_m[gr], bd_hh)) for gr in groups}
    t_b = {gr: t_m[gr].astype(BF16) for gr in groups}
    kq2 = {gr: _dot(t_b[gr], bdiag(kq[gr], bd_hg)) for gr in groups}
    w2 = {gr: _dot(t_b[gr], bdiag(w_m[gr], bd_hg)) for gr in groups}
    ar = {gr: jnp.where(incl, -p1[gr][chunk:], 0.0).astype(BF16) for gr in groups}
    br = {gr: jnp.where(incl, p2[gr][chunk:], 0.0).astype(BF16) for gr in groups}
    rq2 = {gr: rq[gr] + _dot(ar[gr], bdiag(kq2[gr], bd_hg)) for gr in groups}
    y0 = {gr: _dot(ar[gr], bdiag(w2[gr], bd_hg)) + _dot(br[gr], v_d[gr]) for gr in groups}
    bke_b = {gr: bke[gr].astype(BF16) for gr in groups}
    m_full = {gr: _dot_tn(kq2[gr].astype(BF16), bke_b[gr]) for gr in groups}
    n_full = {gr: _dot_tn(jnp.concatenate([tkv[gr][0], -w2[gr]], axis=0).astype(BF16),
                          jnp.concatenate([tkv[gr][1], bke[gr]], axis=0).astype(BF16)) for gr in groups}

    s_old = {gr: s_scr[gr[0], gr[1]] for gr in groups}
    s_d = {gr: bdiag(s_old[gr], bd_gg) for gr in groups}
    y = {gr: _dot_nt(rq2[gr].astype(BF16), s_d[gr]) + y0[gr] for gr in groups}
    s_new = {}
    for gr in groups:
        n_h = jnp.zeros((N_B, gw), F32)
        for i in range(4):
            n_h = n_h + jnp.where(lane_head == i, n_full[gr][i * N_B:(i + 1) * N_B, :], 0.0)
        s_new[gr] = (s_old[gr] * g_last[gr]
                     - _dot(s_old[gr].astype(BF16), jnp.where(bd_gg, m_full[gr], 0.0).astype(BF16)) + n_h)
        s_scr[gr[0], gr[1]] = s_new[gr]

    @pl.when(c == pl.num_programs(1) - 1)
    def _():
        for j, g in groups:
            for i in range(4):
                s_out_ref[j, 4 * g + i] = s_new[(j, g)][:, i * N_B:(i + 1) * N_B]

    def head_means(xs):
        pieces = []
        for gr in groups:
            hi = xs[gr].astype(BF16).astype(F32)
            pieces += [hi, xs[gr] - hi]
        tot = _dot(jnp.concatenate(pieces, axis=0).astype(BF16), ones_gg) * (1.0 / N_B)
        return {gr: tot[2 * i * chunk:(2 * i + 1) * chunk] + tot[(2 * i + 1) * chunk:(2 * i + 2) * chunk]
                for i, gr in enumerate(groups)}

    mean = head_means(y)
    dev = {gr: y[gr] - mean[gr] for gr in groups}
    var = head_means({gr: dev[gr] * dev[gr] for gr in groups})
    for j in range(n_sb):
        outs = [dev[(j, g)] * lax.rsqrt(var[(j, g)] + GN_EPS) for g in range(n_g)]
        yn = jnp.concatenate(outs, axis=-1) * lnw_ref[...] + lnb_ref[...]
        o_ref[j] = (yn + bonus_ref[j]) * g_ref[j]


def _rwkv_scan(streams, ln_w, ln_b, s0, chunk, n_sb):
    _, n_seq, t, _ = streams.shape
    n_chunks = t // chunk
    tri = jnp.tril(jnp.ones((chunk, chunk), F32)).astype(BF16)
    row = pl.BlockSpec((n_sb, chunk, D_B), lambda b, c: (b, c, 0))
    st = pl.BlockSpec((n_sb, H_B, N_B, N_B), lambda b, c: (b, 0, 0, 0))
    stream = lambda k: pl.BlockSpec((None, n_sb, chunk, D_B), lambda b, c: (k, b, c, 0))
    return pl.pallas_call(
        functools.partial(_rwkv_scan_kernel, chunk=chunk, n_sb=n_sb),
        grid=(n_seq // n_sb, n_chunks),
        in_specs=[stream(k) for k in range(N_STREAMS)] + [_full(ln_w.shape), _full(ln_b.shape), st, _full(tri.shape)],
        out_specs=[row, st],
        out_shape=[jax.ShapeDtypeStruct((n_seq, t, D_B), F32), jax.ShapeDtypeStruct(s0.shape, F32)],
        scratch_shapes=[pltpu.VMEM((n_sb, H_B // 4, N_B, 4 * N_B), F32)],
        compiler_params=_cparams(("parallel", "arbitrary")),
        name="rwkv_scan",
    )(*([streams] * N_STREAMS), ln_w, ln_b, s0, tri)


def _pool_prompt_kernel(x_ref, gpre_ref, wpool_ref, pscale_ref, gpost_ref, o_ref, tail_ref, hext, *, tm):
    j = pl.program_id(1)
    x = x_ref[...]
    h = _rms(x, gpre_ref[...])

    @pl.when(j == 0)
    def _():
        hext[0:W_MAX, :] = jnp.zeros((W_MAX, D_MODEL), F32)

    @pl.when(j > 0)
    def _():
        hext[0:W_MAX, :] = hext[tm:tm + W_MAX, :]

    hext[W_MAX:, :] = h
    tail_ref[0] = h[tm - W_MAX:, :]
    pos = j * tm + lax.broadcasted_iota(jnp.int32, (tm, 1), 0)
    acc = hext[...]
    ys = []
    for gi, w in enumerate(POOL_WINDOWS):
        acc = acc[:, POOL_GC * (1 if gi else 0):]
        acc = acc + pltpu.roll(acc, w // 2, 0)
        cnt = jnp.minimum(pos + 1, w).astype(F32)
        pooled = acc[W_MAX:, :POOL_GC] / cnt - h[:, gi * POOL_GC:(gi + 1) * POOL_GC]
        ys.append(_dot(pooled.astype(BF16), wpool_ref[gi]))
    y = jnp.concatenate(ys, axis=-1) * pscale_ref[...]
    o_ref[...] = x + _rms(y, gpost_ref[...])


def _pool_prompt(x, g_pre, w_pool, p_scale, g_post, n_b, t_p):
    m = x.shape[0]
    tm = POOL_TILE
    n_t = t_p // tm
    row = pl.BlockSpec((tm, D_MODEL), lambda b, j: (b * n_t + j, 0))
    return pl.pallas_call(
        functools.partial(_pool_prompt_kernel, tm=tm),
        grid=(n_b, n_t),
        in_specs=[row, _full(g_pre.shape), _full(w_pool.shape), _full(p_scale.shape), _full(g_post.shape)],
        out_specs=[row, pl.BlockSpec((1, W_MAX, D_MODEL), lambda b, j: (b, 0, 0))],
        out_shape=[jax.ShapeDtypeStruct((m, D_MODEL), F32), jax.ShapeDtypeStruct((n_b, W_MAX, D_MODEL), F32)],
        scratch_shapes=[pltpu.VMEM((W_MAX + tm, D_MODEL), F32)],
        compiler_params=_cparams(("parallel", "arbitrary")),
        name="pool_prompt",
    )(x, g_pre, w_pool, p_scale, g_post)


def _pool_sample_kernel(x_ref, pre_ref, gpre_ref, wpool_ref, pscale_ref, gpost_ref, o_ref, h_ref, *, n_b, t_s):
    x = x_ref[...]
    h = _rms(x, gpre_ref[...])
    h_ref[...] = h
    n_pre = W_MAX - 1
    rows = [pre_ref[i * n_b:(i + 1) * n_b, :] for i in range(n_pre)] + [h[t * n_b:(t + 1) * n_b, :] for t in range(t_s)]
    outs = []
    for t in range(t_s):
        ys = []
        for gi, w in enumerate(POOL_WINDOWS):
            sl = slice(gi * POOL_GC, (gi + 1) * POOL_GC)
            win = rows[n_pre + t][:, sl]
            for d in range(1, w):
                win = win + rows[n_pre + t - d][:, sl]
            pooled = win / float(w) - rows[n_pre + t][:, sl]
            ys.append(_dot(pooled.astype(BF16), wpool_ref[gi]))
        outs.append(jnp.concatenate(ys, axis=-1))
    y = jnp.concatenate(outs, axis=0) * pscale_ref[...]
    o_ref[...] = x + _rms(y, gpost_ref[...])


def _pool_sample(x, prefix, g_pre, w_pool, p_scale, g_post, n_b, t_s):
    args = (x, prefix, g_pre, w_pool, p_scale, g_post)
    return pl.pallas_call(
        functools.partial(_pool_sample_kernel, n_b=n_b, t_s=t_s),
        grid=(1,),
        in_specs=[_full(a.shape) for a in args],
        out_specs=[_full(x.shape), _full(x.shape)],
        out_shape=[jax.ShapeDtypeStruct(x.shape, F32)] * 2,
        compiler_params=_cparams(("arbitrary",)),
        name="pool_sample",
    )(*args)


def _rope_tables(pos):
    half = ROPE_DIM // 2
    inv = ROPE_BASE ** (-jnp.arange(half, dtype=F32) / half)
    ang = pos[:, None] * inv[None, :]
    cos, sin = jnp.cos(ang), jnp.sin(ang)
    reps = LANES // ROPE_DIM
    return jnp.tile(jnp.concatenate([cos, cos], -1), (1, reps)), jnp.tile(jnp.concatenate([-sin, sin], -1), (1, reps))


def _swap_halves(w):
    half = ROPE_DIM // 2
    return jnp.concatenate([w[..., half:], w[..., :half]], axis=-1)


def _ab_weights(e, w_in, g_q, w_uq, g_kv, w_uk):
    w = w_in[e]
    w_pe = w[:, Q_LORA + KV_LORA:MLA_IN]
    wq2 = jnp.concatenate([w[:, :Q_LORA], w_pe, _swap_halves(w_pe)], axis=1).astype(BF16)
    wkv = w[:, Q_LORA:Q_LORA + KV_LORA].astype(BF16)
    wb = w[:, MLA_IN:].astype(BF16)
    uq = w_uq[e].reshape(Q_LORA, H_A, NOPE_DIM + ROPE_DIM)
    uq_pe = uq[:, :, NOPE_DIM:]
    wuq = jnp.concatenate([uq[:, :, :NOPE_DIM].reshape(Q_LORA, H_A * NOPE_DIM),
                           uq_pe.reshape(Q_LORA, H_A * ROPE_DIM),
                           _swap_halves(uq_pe).reshape(Q_LORA, H_A * ROPE_DIM)], axis=1).astype(BF16)
    wuk = jnp.transpose(w_uk[e], (1, 2, 0)).astype(BF16)
    src = jnp.arange(H_A * ROPE_DIM)
    dst = (src // ROPE_DIM) * LANES + src % ROPE_DIM
    spread = (dst[:, None] == jnp.arange(H_A * LANES)[None, :]).astype(BF16)
    return wq2, wkv, wb, g_q[e][None], g_kv[e][None], wuq, wuk, spread


def _rwkv_weights(e, mu_shift, w0, w2, a0, a2, g2, k_k, k_a, r_k):
    head = jnp.arange(D_B) // N_B
    ones = (head[:, None] == head[None, :]).astype(BF16)
    return (mu_shift[e][None], w0[e][None], w2[e], a0[e][None], a2[e], g2[e],
            k_k[e][None], k_a[e][None], r_k[e].reshape(1, D_B), ones)


def kernel(x_prompt, x_sample, cache_ckv, cache_kpe, page_table, state_wkv, state_shift, state_pool, meta_tokens,
           g_mix_pre, g_mix_post, g_ffn_pre, g_ffn_post, w_in, g_q, w_uq, g_kv, w_uk, w_uv, mu_shift, w0, w2, a0, a2,
           g2, k_k, k_a, r_k, ln_w, ln_b, w_out, w_pool, pool_scale, w_up, w_down):
    n_bp, seq, _ = x_prompt.shape
    n_bs, t_s, _ = x_sample.shape
    depth = g_mix_pre.shape[0]
    t_p = seq + N_META
    n_pages = page_table.shape[1]
    past = n_pages * PAGE_SIZE
    m_p = n_bp * t_p
    m_s = n_bs * t_s

    meta = jnp.broadcast_to(meta_tokens[None].astype(x_prompt.dtype), (n_bp, N_META, D_MODEL))
    xp = jnp.concatenate([meta, x_prompt], axis=1).reshape(m_p, D_MODEL)
    xs = jnp.transpose(x_sample, (1, 0, 2)).reshape(m_s, D_MODEL)

    tm_s = m_s
    cos_p, sin_p = _rope_tables((jnp.arange(t_p + ROW_TILE) % t_p).astype(F32))
    cos_s, sin_s = _rope_tables((past + (jnp.arange(m_s + tm_s) % m_s) // n_bs).astype(F32))
    cache_kpe_t = jnp.swapaxes(cache_kpe, 2, 3)

    ckv_p, kpe_p, wkv_p, shift_p, pool_p = [], [], [], [], []
    ckv_s, kpe_s, wkv_s, shift_s, pool_s = [], [], [], [], []
    for l in range(depth):
        g_pre, g_post = g_mix_pre[l][None], g_mix_post[l][None]
        if l % 2 == 0:
            e = l // 2
            ab_w = _ab_weights(e, w_in, g_q, w_uq, g_kv, w_uk)
            rw = _rwkv_weights(e, mu_shift, w0, w2, a0, a2, g2, k_k, k_a, r_k)
            wuv = jnp.transpose(w_uv[e], (1, 0, 2)).astype(BF16)
            wout = w_out[e].astype(BF16)
            lnw, lnb = ln_w[e][None], ln_b[e][None]

            ckv, kpe, zb, qabs, kvb = _ab_in_proj(xp, g_pre, ab_w, cos_p, sin_p, ROW_TILE)
            olat = _attn_prompt(qabs, kvb, n_bp, t_p)
            zb3 = zb.reshape(n_bp, t_p, RWKV_IN)
            first = jnp.zeros((n_bp, 8, RWKV_IN), F32)
            streams = _rwkv_prep(zb, first, rw, PREP_TILE, 1).reshape(N_STREAMS, n_bp, t_p, D_B)
            o_b, s_fin = _rwkv_scan(streams, lnw, lnb, jnp.zeros((n_bp, H_B, N_B, N_B), F32), SCAN_CHUNK_P,
                                    SCAN_SEQS_P)
            xp = _ab_out_proj(olat, o_b.reshape(m_p, D_B), xp, wuv, wout, g_post, ROW_TILE)
            ckv_p.append(ckv.reshape(n_bp, t_p, KV_LORA))
            kpe_p.append(kpe.reshape(n_bp, t_p, ROPE_DIM))
            wkv_p.append(s_fin)
            shift_p.append(zb3[:, -1])

            ckv, kpe, zb, qabs, kvb = _ab_in_proj(xs, g_pre, ab_w, cos_s, sin_s, tm_s)
            q_s = jnp.transpose(qabs.reshape(H_A, t_s, n_bs, QK_PAD), (2, 0, 1, 3)).reshape(n_bs, H_A * t_s, QK_PAD)
            k_new = jnp.pad(jnp.transpose(kvb.reshape(t_s, n_bs, QK_PAD), (1, 0, 2)),
                            ((0, 0), (0, NEW_KEY_ROWS - t_s), (0, 0)))
            o_s = _attn_sample(page_table, q_s, k_new, cache_ckv, cache_kpe_t, e, t_s)
            olat = jnp.transpose(o_s.reshape(n_bs, H_A, t_s, KV_LORA), (1, 2, 0, 3)).reshape(H_A, m_s, KV_LORA)
            streams = _rwkv_prep(zb, state_shift[e].astype(F32)[None], rw, tm_s, n_bs)
            streams = jnp.pad(jnp.transpose(streams.reshape(N_STREAMS, t_s, n_bs, D_B), (0, 2, 1, 3)),
                              ((0, 0), (0, 0), (0, SCAN_CHUNK_S - t_s), (0, 0)))
            o_b, s_fin = _rwkv_scan(streams, lnw, lnb, state_wkv[e].astype(F32), SCAN_CHUNK_S, SCAN_SEQS_S)
            o_b = jnp.transpose(o_b[:, :t_s], (1, 0, 2)).reshape(m_s, D_B)
            xs = _ab_out_proj(olat, o_b, xs, wuv, wout, g_post, tm_s)
            ckv_s.append(jnp.transpose(ckv.reshape(t_s, n_bs, KV_LORA), (1, 0, 2)))
            kpe_s.append(jnp.transpose(kpe.reshape(t_s, n_bs, ROPE_DIM), (1, 0, 2)))
            wkv_s.append(s_fin)
            shift_s.append(zb[m_s - n_bs:])
        else:
            o = l // 2
            wp = w_pool[o].astype(BF16)
            ps = pool_scale[o][None]
            xp, tail = _pool_prompt(xp, g_pre, wp, ps, g_post, n_bp, t_p)
            pool_p.append(tail[:, 1:])
            prefix = jnp.transpose(state_pool[o].astype(F32), (1, 0, 2))
            xs, h_s = _pool_sample(xs, prefix.reshape((W_MAX - 1) * n_bs, D_MODEL), g_pre, wp, ps, g_post, n_bs, t_s)
            full = jnp.concatenate([prefix, h_s.reshape(t_s, n_bs, D_MODEL)], axis=0)
            pool_s.append(jnp.transpose(full[-(W_MAX - 1):], (1, 0, 2)))
        gfp, gfo = g_ffn_pre[l][None], g_ffn_post[l][None]
        xp, xs = _ffn(xp, xs, gfp, w_up, w_down, gfo, l, ROW_TILE)

    y_prompt = xp.reshape(n_bp, t_p, D_MODEL)[:, N_META:]
    y_sample = jnp.transpose(xs.reshape(t_s, n_bs, D_MODEL), (1, 0, 2))
    return (y_prompt, y_sample, jnp.stack(ckv_p), jnp.stack(kpe_p), jnp.stack(wkv_p), jnp.stack(shift_p),
            jnp.stack(pool_p), jnp.stack(ckv_s), jnp.stack(kpe_s), jnp.stack(wkv_s), jnp.stack(shift_s),
            jnp.stack(pool_s))
```

```python
import functools

import jax
import jax.numpy as jnp
from jax import lax
from jax.experimental import pallas as pl
from jax.experimental.pallas import tpu as pltpu

F32 = jnp.float32
BF16 = jnp.bfloat16

D_MODEL = 1024
N_META = 16
PAGE_SIZE = 128
H_A = 8
Q_LORA = 384
KV_LORA = 256
NOPE_DIM = 64
ROPE_DIM = 32
V_DIM = 64
ROPE_BASE = 10000.0
SM_SCALE = (NOPE_DIM + ROPE_DIM) ** -0.5
LOG2_E = 1.4426950408889634
H_B = 8
N_B = 64
D_B = H_B * N_B
DECAY_LORA = 64
AAA_LORA = 64
GATE_LORA = 160
RWKV_IN = 3 * D_B + DECAY_LORA + AAA_LORA + GATE_LORA
MLA_IN = Q_LORA + KV_LORA + ROPE_DIM
GN_EPS = 64e-5
POOL_WINDOWS = (2, 4, 8, 16)
POOL_GC = D_MODEL // len(POOL_WINDOWS)
W_MAX = 16
D_FF = 4 * D_MODEL
RMS_EPS = 1e-6

LANES = 128
QK_PAD = KV_LORA + LANES
VMEM_LIMIT = 56 * 1024 * 1024
ROW_TILE = 384
ATT_Q_TILE = 256
ATT_SCORE_LAG = 2
POOL_TILE = 688
PREP_TILE = 344
SCAN_CHUNK_P = 48
SCAN_CHUNK_S = 8
SCAN_SEQS_P = 8
SCAN_SEQS_S = 16
NEW_KEY_ROWS = 16
N_STREAMS = 8
PAGES_PER_CHUNK = 8
FFN_STAGE_BYTES = 2 * 1024 * 1024


def _cparams(sem):
    return pltpu.CompilerParams(dimension_semantics=sem, vmem_limit_bytes=VMEM_LIMIT)


def _rms(x, g):
    return x * lax.rsqrt(jnp.mean(x * x, axis=-1, keepdims=True) + RMS_EPS) * g


def _dot(a, b):
    return jnp.dot(a, b, preferred_element_type=F32)


def _dot_nt(a, b):
    return lax.dot_general(a, b, (((1,), (1,)), ((), ())), preferred_element_type=F32)


def _dot_tn(a, b):
    return lax.dot_general(a, b, (((0,), (0,)), ((), ())), preferred_element_type=F32)


def _full(shape):
    n = len(shape)
    return pl.BlockSpec(shape, lambda *_: (0,) * n)


def _ab_in_kernel(x_ref, gpre_ref, wq2_ref, wkv_ref, wb_ref, gq_ref, gkv_ref, wuq_ref, wuk_ref, spread_ref, cos_ref, sin_ref,
                  ckv_ref, kpe_ref, zb_ref, qabs_ref, kvb_ref, *, period):
    h = _rms(x_ref[...], gpre_ref[...]).astype(BF16)
    zq2 = _dot(h, wq2_ref[...])
    zkv = _dot(h, wkv_ref[...])
    zb_ref[...] = _dot(h, wb_ref[...])
    ckv = _rms(zkv, gkv_ref[...])
    ckv_ref[...] = ckv
    tm = x_ref.shape[0]
    start = pl.multiple_of((pl.program_id(0) * tm) % period, 8)
    cos = cos_ref[pl.ds(start, tm), :]
    sin = sin_ref[pl.ds(start, tm), :]
    pe = zq2[:, Q_LORA:Q_LORA + ROPE_DIM]
    pe_sw = zq2[:, Q_LORA + ROPE_DIM:Q_LORA + 2 * ROPE_DIM]
    kpe = pe * cos[:, :ROPE_DIM] + pe_sw * sin[:, :ROPE_DIM]
    kpe_ref[...] = kpe
    kvb_ref[:, :KV_LORA] = ckv.astype(BF16)
    kvb_ref[:, KV_LORA:] = jnp.concatenate([kpe, jnp.zeros((kpe.shape[0], LANES - ROPE_DIM), F32)], axis=1).astype(BF16)
    qn = _rms(zq2[:, :Q_LORA], gq_ref[...]).astype(BF16)
    q2 = _dot(qn, wuq_ref[...])
    n_nope = H_A * NOPE_DIM
    n_pe = H_A * ROPE_DIM
    reps = n_pe // LANES
    q_pe = (q2[:, n_nope:n_nope + n_pe] * jnp.concatenate([cos] * reps, axis=1)
            + q2[:, n_nope + n_pe:] * jnp.concatenate([sin] * reps, axis=1)).astype(BF16)
    slabs = _dot(q_pe, spread_ref[...]).astype(BF16)
    for hd in range(H_A):
        qn_h = q2[:, hd * NOPE_DIM:(hd + 1) * NOPE_DIM].astype(BF16)
        qabs_ref[hd, :, 0:KV_LORA] = _dot(qn_h, wuk_ref[hd]).astype(BF16)
        qabs_ref[hd, :, KV_LORA:QK_PAD] = slabs[:, hd * LANES:(hd + 1) * LANES]


def _ab_in_proj(x, g_pre, wts, cos_t, sin_t, tm):
    m = x.shape[0]
    wq2, wkv, wb, g_q, g_kv, wuq, wuk, spread = wts
    row = lambda w: pl.BlockSpec((tm, w), lambda i: (i, 0))
    return pl.pallas_call(
        functools.partial(_ab_in_kernel, period=cos_t.shape[0] - tm),
        grid=(m // tm,),
        in_specs=[row(D_MODEL), _full(g_pre.shape), _full(wq2.shape), _full(wkv.shape), _full(wb.shape),
                  _full(g_q.shape), _full(g_kv.shape), _full(wuq.shape), _full(wuk.shape), _full(spread.shape),
                  _full(cos_t.shape),
                  _full(sin_t.shape)],
        out_specs=[row(KV_LORA), row(ROPE_DIM), row(RWKV_IN),
                   pl.BlockSpec((H_A, tm, QK_PAD), lambda i: (0, i, 0)), row(QK_PAD)],
        out_shape=[jax.ShapeDtypeStruct((m, KV_LORA), F32), jax.ShapeDtypeStruct((m, ROPE_DIM), F32),
                   jax.ShapeDtypeStruct((m, RWKV_IN), F32), jax.ShapeDtypeStruct((H_A, m, QK_PAD), BF16),
                   jax.ShapeDtypeStruct((m, QK_PAD), BF16)],
        compiler_params=_cparams(("parallel",)),
        name="ab_in_proj",
    )(x, g_pre, wq2, wkv, wb, g_q, g_kv, wuq, wuk, spread, cos_t, sin_t)


def _attn_prompt_kernel(q_ref, kv_ref, o_ref, kpad, *, t_p, tq):
    t_pad = kpad.shape[0]

    @pl.when(pl.program_id(1) == 0)
    def _():
        kpad[:t_p, :] = kv_ref[0]
        kpad[t_p:, :] = jnp.zeros((t_pad - t_p, QK_PAD), BF16)

    n_t = t_p // tq
    tiles = [(i * tq, t_p - i * tq if i == n_t - 1 else tq) for i in range(n_t)]
    kext = [min(-(-(q0 + rows) // LANES) * LANES, t_pad) for q0, rows in tiles]
    scores = {}
    lag = ATT_SCORE_LAG
    for i in range(n_t + lag):
        if i < n_t:
            q0, rows = tiles[i]
            scores[i] = _dot_nt(q_ref[0, q0:q0 + rows, :], kpad[:kext[i], :])
        if i >= lag:
            q0, rows = tiles[i - lag]
            ke = kext[i - lag]
            s = scores.pop(i - lag)
            lo = (q0 // LANES) * LANES
            q_pos = q0 + lax.broadcasted_iota(jnp.int32, (rows, ke - lo), 0)
            k_pos = lo + lax.broadcasted_iota(jnp.int32, (rows, ke - lo), 1)
            s_diag = jnp.where(k_pos <= q_pos, s[:, lo:], -jnp.inf)
            s = jnp.concatenate([s[:, :lo], s_diag], axis=1) if lo else s_diag
            p = jnp.exp2((s - jnp.max(s, axis=-1, keepdims=True)) * (SM_SCALE * LOG2_E))
            l = jnp.sum(p, axis=-1, keepdims=True)
            o = _dot(p.astype(BF16), kpad[:ke, :KV_LORA])
            o_ref[0, q0:q0 + rows, :] = (o / l).astype(BF16)


def _attn_prompt(qabs, kvb, n_b, t_p):
    m = qabs.shape[1]
    t_pad = -(-t_p // LANES) * LANES
    return pl.pallas_call(
        functools.partial(_attn_prompt_kernel, t_p=t_p, tq=ATT_Q_TILE),
        grid=(n_b, H_A),
        in_specs=[pl.BlockSpec((1, t_p, QK_PAD), lambda b, h: (h, b, 0)),
                  pl.BlockSpec((1, t_p, QK_PAD), lambda b, h: (b, 0, 0))],
        out_specs=pl.BlockSpec((1, t_p, KV_LORA), lambda b, h: (h, b, 0)),
        out_shape=jax.ShapeDtypeStruct((H_A, m, KV_LORA), BF16),
        scratch_shapes=[pltpu.VMEM((t_pad, QK_PAD), BF16)],
        compiler_params=_cparams(("parallel", "arbitrary")),
        name="attn_prompt",
    )(qabs, kvb.reshape(n_b, t_p, QK_PAD))


def _attn_sample_kernel(pt_ref, q_ref, knew_ref, ckv_hbm, kpe_hbm, o_ref, ckv_buf, kpe_buf, sems, *, layer, n_pages,
                        t_s):
    b = pl.program_id(0)
    n_b = pl.num_programs(0)
    slot = b % 2

    def page_copies(bb, sl, p):
        page = pt_ref[bb, p]
        return (pltpu.make_async_copy(ckv_hbm.at[layer, page], ckv_buf.at[sl, p], sems.at[0, sl]),
                pltpu.make_async_copy(kpe_hbm.at[layer, page], kpe_buf.at[sl, p], sems.at[1, sl]))

    def start_all(bb, sl):
        for p in range(n_pages):
            for cp in page_copies(bb, sl, p):
                cp.start()

    @pl.when(b == 0)
    def _():
        start_all(0, 0)

    @pl.when(b + 1 < n_b)
    def _():
        start_all(b + 1, 1 - slot)

    for p in range(n_pages):
        for cp in page_copies(b, slot, p):
            cp.wait()

    q = q_ref[0]
    q_lat = q[:, :KV_LORA]
    q_pe = q[:, KV_LORA:KV_LORA + ROPE_DIM]
    rows = PAGES_PER_CHUNK * PAGE_SIZE
    n_chunks = n_pages // PAGES_PER_CHUNK

    def latent_rows(c):
        return ckv_buf[slot, c * PAGES_PER_CHUNK:(c + 1) * PAGES_PER_CHUNK].reshape(rows, KV_LORA).astype(BF16)

    parts = []
    scores = {}
    lag = ATT_SCORE_LAG
    for c in range(n_chunks + lag):
        if c < n_chunks:
            kp = jnp.concatenate([kpe_buf[slot, c * PAGES_PER_CHUNK + i] for i in range(PAGES_PER_CHUNK)], axis=1)
            scores[c] = (_dot_nt(q_lat, latent_rows(c)) + _dot(q_pe, kp.astype(BF16))) * SM_SCALE
        if c >= lag:
            s = scores.pop(c - lag)
            m_c = jnp.max(s, axis=-1, keepdims=True)
            p = jnp.exp(s - m_c)
            parts.append((m_c, jnp.sum(p, axis=-1, keepdims=True), _dot(p.astype(BF16), latent_rows(c - lag))))
    kn = knew_ref[0]
    s_new = _dot_nt(q, kn) * SM_SCALE
    step = lax.broadcasted_iota(jnp.int32, s_new.shape, 0) % t_s
    key = lax.broadcasted_iota(jnp.int32, s_new.shape, 1)
    s_new = jnp.where(key <= step, s_new, -jnp.inf)
    m_n = jnp.max(s_new, axis=-1, keepdims=True)
    p_new = jnp.exp(s_new - m_n)
    parts.append((m_n, jnp.sum(p_new, axis=-1, keepdims=True), _dot(p_new.astype(BF16), kn[:, :KV_LORA])))
    m = parts[0][0]
    for m_c, _, _ in parts[1:]:
        m = jnp.maximum(m, m_c)
    l = jnp.zeros_like(m)
    acc = jnp.zeros((q.shape[0], KV_LORA), F32)
    for m_c, l_c, acc_c in parts:
        w_c = jnp.exp(m_c - m)
        l = l + w_c * l_c
        acc = acc + w_c * acc_c
    o_ref[0] = (acc / l).astype(BF16)


def _attn_sample(page_table, q_s, k_new, cache_ckv, cache_kpe, layer, t_s):
    n_b, n_pages = page_table.shape
    n_q = q_s.shape[1]
    grid_spec = pltpu.PrefetchScalarGridSpec(
        num_scalar_prefetch=1,
        grid=(n_b,),
        in_specs=[pl.BlockSpec((1, n_q, QK_PAD), lambda b, pt: (b, 0, 0)),
                  pl.BlockSpec((1,) + k_new.shape[1:], lambda b, pt: (b, 0, 0)),
                  pl.BlockSpec(memory_space=pl.ANY),
                  pl.BlockSpec(memory_space=pl.ANY)],
        out_specs=pl.BlockSpec((1, n_q, KV_LORA), lambda b, pt: (b, 0, 0)),
        scratch_shapes=[pltpu.VMEM((2, n_pages, PAGE_SIZE, KV_LORA), F32),
                        pltpu.VMEM((2, n_pages, ROPE_DIM, PAGE_SIZE), F32),
                        pltpu.SemaphoreType.DMA((2, 2))],
    )
    return pl.pallas_call(
        functools.partial(_attn_sample_kernel, layer=layer, n_pages=n_pages, t_s=t_s),
        grid_spec=grid_spec,
        out_shape=jax.ShapeDtypeStruct((n_b, n_q, KV_LORA), BF16),
        compiler_params=_cparams(("arbitrary",)),
        name="attn_sample",
    )(page_table, q_s, k_new, cache_ckv, cache_kpe)


def _ab_out_kernel(olat_ref, ob_ref, x_ref, wuv_ref, wout_ref, gpost_ref, o_ref, cat_ref):
    n_a = H_A * V_DIM
    for hd in range(0, H_A, 2):
        pair = [_dot(olat_ref[hd + i], wuv_ref[hd + i]) for i in range(2)]
        cat_ref[:, hd * V_DIM:(hd + 2) * V_DIM] = jnp.concatenate(pair, axis=1).astype(BF16)
    cat_ref[:, n_a:] = ob_ref[...].astype(BF16)
    y = _dot(cat_ref[...], wout_ref[...])
    o_ref[...] = x_ref[...] + _rms(y, gpost_ref[...])


def _ab_out_proj(olat, o_b, x, wuv, wout, g_post, tm):
    m = x.shape[0]
    row = lambda w: pl.BlockSpec((tm, w), lambda i: (i, 0))
    return pl.pallas_call(
        _ab_out_kernel,
        grid=(m // tm,),
        in_specs=[pl.BlockSpec((H_A, tm, KV_LORA), lambda i: (0, i, 0)), row(D_B), row(D_MODEL),
                  _full(wuv.shape), _full(wout.shape), _full(g_post.shape)],
        out_specs=row(D_MODEL),
        out_shape=jax.ShapeDtypeStruct((m, D_MODEL), F32),
        scratch_shapes=[pltpu.VMEM((tm, H_A * V_DIM + D_B), BF16)],
        compiler_params=_cparams(("parallel",)),
        name="ab_out_proj",
    )(olat, o_b, x, wuv, wout, g_post)


def _ffn_kernel(xp_ref, xs_ref, gpre_ref, wup_hbm, wdown_hbm, gpost_ref, op_ref, os_ref, wup_ref, wdown_ref, st_up,
                st_dn, sems, *, ff_chunk, layer, n_p):
    @pl.when(pl.program_id(0) == 0)
    def _():
        jobs = []
        for k, (src, dst, st) in enumerate(((wup_hbm, wup_ref, st_up), (wdown_hbm, wdown_ref, st_dn))):
            rows = st.shape[1]
            for c in range(dst.shape[0] // rows):
                jobs.append((k, c, rows, src, dst, st))

        def copy(job, slot):
            k, c, rows, src, _, st = job
            return pltpu.make_async_copy(src.at[layer, c * rows:(c + 1) * rows], st.at[slot], sems.at[k, slot])

        copy(jobs[0], 0).start()
        for i, job in enumerate(jobs):
            slot = i % 2
            if i + 1 < len(jobs):
                copy(jobs[i + 1], 1 - slot).start()
            copy(job, slot).wait()
            k, c, rows, _, dst, st = job
            dst[c * rows:(c + 1) * rows, :] = st[slot].astype(BF16)

    def mlp(x_ref, o_ref):
        x = x_ref[...]
        h = _rms(x, gpre_ref[...]).astype(BF16)
        y = jnp.zeros(x.shape, F32)
        for c in range(D_FF // ff_chunk):
            u = _dot(h, wup_ref[:, c * ff_chunk:(c + 1) * ff_chunk])
            a = jnp.square(jnp.maximum(u, 0.0)).astype(BF16)
            y = y + _dot(a, wdown_ref[c * ff_chunk:(c + 1) * ff_chunk, :])
        o_ref[...] = x + _rms(y, gpost_ref[...])

    @pl.when(pl.program_id(0) < n_p)
    def _():
        mlp(xp_ref, op_ref)

    @pl.when(pl.program_id(0) == n_p)
    def _():
        mlp(xs_ref, os_ref)


def _ffn(xp, xs, g_pre, w_up, w_down, g_post, layer, tm):
    n_p = xp.shape[0] // tm
    row = pl.BlockSpec((tm, D_MODEL), lambda i: (jnp.minimum(i, n_p - 1), 0))
    return pl.pallas_call(
        functools.partial(_ffn_kernel, ff_chunk=1024, layer=layer, n_p=n_p),
        grid=(n_p + 1,),
        in_specs=[row, _full(xs.shape), _full(g_pre.shape), pl.BlockSpec(memory_space=pl.ANY),
                  pl.BlockSpec(memory_space=pl.ANY), _full(g_post.shape)],
        out_specs=[row, _full(xs.shape)],
        out_shape=[jax.ShapeDtypeStruct(xp.shape, F32), jax.ShapeDtypeStruct(xs.shape, F32)],
        scratch_shapes=[pltpu.VMEM(w_up.shape[1:], BF16), pltpu.VMEM(w_down.shape[1:], BF16),
                        pltpu.VMEM((2, FFN_STAGE_BYTES // (4 * D_FF), D_FF), F32),
                        pltpu.VMEM((2, FFN_STAGE_BYTES // (4 * D_MODEL), D_MODEL), F32),
                        pltpu.SemaphoreType.DMA((2, 2))],
        compiler_params=_cparams(("arbitrary",)),
        name="ffn",
    )(xp, xs, g_pre, w_up, w_down, g_post)


def _split_dot(x, w):
    hi = x.astype(BF16)
    lo = (x - hi.astype(F32)).astype(BF16)
    return _dot(hi, w) + _dot(lo, w)


def _rwkv_prep_kernel(zb_ref, first_ref, mu_ref, w0_ref, w2_ref, a0_ref, a2_ref, g2_ref, kk_ref, ka_ref, rk_ref,
                      ones_ref, out_ref, carry, *, shift):
    @pl.when(pl.program_id(1) == 0)
    def _():
        carry[...] = first_ref[0]

    zb = zb_ref[...]
    tm = zb.shape[0]
    if shift == 1:
        row = lax.broadcasted_iota(jnp.int32, (tm, 1), 0)
        prev = jnp.where(row == 0, carry[0:1, :], pltpu.roll(zb, 1, 0))
        carry[0:1, :] = zb[tm - 1:tm, :]
    else:
        prev = jnp.concatenate([carry[...], zb[:tm - shift]], axis=0)
        carry[...] = zb[tm - shift:]
    xm = zb + (prev - zb) * mu_ref[...]
    o3 = 3 * D_B
    o4 = o3 + DECAY_LORA
    o5 = o4 + AAA_LORA
    r = xm[:, :D_B]
    k = xm[:, D_B:2 * D_B]
    v = xm[:, 2 * D_B:o3]
    z = -(w0_ref[...] + _dot(jnp.tanh(xm[:, o3:o4]).astype(BF16), w2_ref[...].astype(BF16)))
    softplus = jnp.maximum(z, 0.0) + jnp.log(1.0 + jnp.exp(-jnp.abs(z)))
    out_ref[1] = -jnp.exp(-softplus - 0.5)
    a = 1.0 / (1.0 + jnp.exp(-(a0_ref[...] + _dot(xm[:, o4:o5].astype(BF16), a2_ref[...].astype(BF16)))))
    sg = 1.0 / (1.0 + jnp.exp(-xm[:, o5:]))
    out_ref[6] = _dot(sg.astype(BF16), g2_ref[...].astype(BF16))
    ones = ones_ref[...]
    kk = k * kk_ref[...]
    kk = kk / jnp.maximum(jnp.sqrt(_split_dot(kk * kk, ones)), 1e-12)
    keff = k * (1.0 + (a - 1.0) * ka_ref[...])
    out_ref[0] = r
    out_ref[2] = keff
    out_ref[3] = v
    out_ref[4] = kk
    out_ref[5] = kk * a
    out_ref[7] = _split_dot(r * keff * rk_ref[...], ones) * v


def _rwkv_prep(zb, first, rw, tm, shift):
    m = zb.shape[0]
    n_seq = first.shape[0]
    n_t = m // (n_seq * tm)
    row = lambda w: pl.BlockSpec((tm, w), lambda b, j: (b * n_t + j, 0))
    return pl.pallas_call(
        functools.partial(_rwkv_prep_kernel, shift=shift),
        grid=(n_seq, n_t),
        in_specs=[row(RWKV_IN), pl.BlockSpec((1,) + first.shape[1:], lambda b, j: (b, 0, 0))]
        + [_full(w.shape) for w in rw],
        out_specs=pl.BlockSpec((N_STREAMS, tm, D_B), lambda b, j: (0, b * n_t + j, 0)),
        out_shape=jax.ShapeDtypeStruct((N_STREAMS, m, D_B), F32),
        scratch_shapes=[pltpu.VMEM(first.shape[1:], F32)],
        compiler_params=_cparams(("parallel", "arbitrary")),
        name="rwkv_prep",
    )(zb, first, *rw)


def _rwkv_scan_kernel(r_ref, lw_ref, keff_ref, v_ref, kn_ref, b_ref, g_ref, bonus_ref, lnw_ref, lnb_ref, s0_ref,
                      tri_ref, o_ref, s_out_ref, s_scr, *, chunk, n_sb):
    c = pl.program_id(1)
    n_g = H_B // 4
    gw = 4 * N_B
    hw = 4 * chunk
    groups = [(j, g) for j in range(n_sb) for g in range(n_g)]

    @pl.when(c == 0)
    def _():
        for j, g in groups:
            s_scr[j, g] = jnp.concatenate([s0_ref[j, 4 * g + i] for i in range(4)], axis=1)

    def blk(idx, size):
        return ((idx >= size).astype(jnp.int32) + (idx >= 2 * size).astype(jnp.int32)
                + (idx >= 3 * size).astype(jnp.int32))

    t_i = lax.broadcasted_iota(jnp.int32, (chunk, hw), 0)
    col = lax.broadcasted_iota(jnp.int32, (chunk, hw), 1)
    s_i = col - chunk * blk(col, chunk)
    strict = s_i < t_i
    incl = s_i <= t_i
    eye = jnp.where(s_i == t_i, 1.0, 0.0).astype(F32)
    levels = []
    k_lvl = 1
    while (1 << (k_lvl - 1)) < chunk:
        same_new = (t_i >> k_lvl) == (s_i >> k_lvl)
        same_old = (t_i >> (k_lvl - 1)) == (s_i >> (k_lvl - 1))
        levels.append(strict & same_new & jnp.logical_not(same_old))
        k_lvl += 1
    bd_hh = (blk(lax.broadcasted_iota(jnp.int32, (hw, hw), 0), chunk)
             == blk(lax.broadcasted_iota(jnp.int32, (hw, hw), 1), chunk))
    bd_hg = (blk(lax.broadcasted_iota(jnp.int32, (hw, gw), 0), chunk)
             == (lax.broadcasted_iota(jnp.int32, (hw, gw), 1) >> 6))
    lane_head = lax.broadcasted_iota(jnp.int32, (N_B, gw), 1) >> 6
    bd_gg = (lax.broadcasted_iota(jnp.int32, (gw, gw), 0) >> 6) == (lax.broadcasted_iota(jnp.int32, (gw, gw), 1) >> 6)
    ones_gg = jnp.where(bd_gg, 1.0, 0.0).astype(BF16)

    def bdiag(x, mask):
        return jnp.where(mask, jnp.concatenate([x] * 4, axis=0), 0.0).astype(BF16)

    tri = tri_ref[...]
    qr, bk_d, kk_d, v_d, kq, rq, bke, tkv, g_last = {}, {}, {}, {}, {}, {}, {}, {}, {}
    for j in range(n_sb):
        lw = lw_ref[j]
        hi = lw.astype(BF16)
        r1 = lw - hi.astype(F32)
        mid = r1.astype(BF16)
        lo = (r1 - mid.astype(F32)).astype(BF16)
        cs = _dot(tri, hi) + _dot(tri, mid) + _dot(tri, lo)
        cs_last = cs[chunk - 1:chunk, :]
        g_inv = jnp.exp(-cs)
        g_end = jnp.exp(cs_last - cs)
        kq_j = kn_ref[j] * jnp.exp(cs - lw)
        rq_j = r_ref[j] * jnp.exp(cs)
        b_j = b_ref[j]
        ke_j = keff_ref[j]
        v_j = v_ref[j]
        bk_j = b_j * g_inv
        kk_j = ke_j * g_inv
        bke_j = b_j * g_end
        kke_j = ke_j * g_end
        gl_j = jnp.exp(cs_last)
        for g in range(n_g):
            sl = slice(g * gw, (g + 1) * gw)
            gr = (j, g)
            kq[gr], rq[gr], g_last[gr] = kq_j[:, sl], rq_j[:, sl], gl_j[:, sl]
            qr[gr] = jnp.concatenate([kq[gr], rq[gr]], axis=0).astype(BF16)
            bk_d[gr], kk_d[gr], v_d[gr] = bdiag(bk_j[:, sl], bd_hg), bdiag(kk_j[:, sl], bd_hg), bdiag(v_j[:, sl], bd_hg)
            bke[gr] = bke_j[:, sl]
            tkv[gr] = (v_j[:, sl], kke_j[:, sl])

    p1 = {gr: _dot_nt(qr[gr], bk_d[gr]) for gr in groups}
    p2 = {gr: _dot_nt(qr[gr], kk_d[gr]) for gr in groups}
    a_m = {gr: -p1[gr][:chunk] for gr in groups}
    t_m = {gr: eye + jnp.where(levels[0], a_m[gr], 0.0) for gr in groups}
    w_m = {gr: _dot(jnp.where(strict, p2[gr][:chunk], 0.0).astype(BF16), v_d[gr]) for gr in groups}
    for lvl in levels[1:]:
        x_m = {gr: _dot(jnp.where(lvl, a_m[gr], 0.0).astype(BF16), bdiag(t_m[gr], bd_hh)) for gr in groups}
        t_m = {gr: t_m[gr] + _dot(t_m[gr].astype(BF16), bdiag(x_m[gr], bd_hh)) for gr in groups}
    t_b = {gr: t_m[gr].astype(BF16) for gr in groups}
    kq2 = {gr: _dot(t_b[gr], bdiag(kq[gr], bd_hg)) for gr in groups}
    w2 = {gr: _dot(t_b[gr], bdiag(w_m[gr], bd_hg)) for gr in groups}
    ar = {gr: jnp.where(incl, -p1[gr][chunk:], 0.0).astype(BF16) for gr in groups}
    br = {gr: jnp.where(incl, p2[gr][chunk:], 0.0).astype(BF16) for gr in groups}
    rq2 = {gr: rq[gr] + _dot(ar[gr], bdiag(kq2[gr], bd_hg)) for gr in groups}
    y0 = {gr: _dot(ar[gr], bdiag(w2[gr], bd_hg)) + _dot(br[gr], v_d[gr]) for gr in groups}
    bke_b = {gr: bke[gr].astype(BF16) for gr in groups}
    m_full = {gr: _dot_tn(kq2[gr].astype(BF16), bke_b[gr]) for gr in groups}
    n_full = {gr: _dot_tn(jnp.concatenate([tkv[gr][0], -w2[gr]], axis=0).astype(BF16),
                          jnp.concatenate([tkv[gr][1], bke[gr]], axis=0).astype(BF16)) for gr in groups}

    s_old = {gr: s_scr[gr[0], gr[1]] for gr in groups}
    s_d = {gr: bdiag(s_old[gr], bd_gg) for gr in groups}
    y = {gr: _dot_nt(rq2[gr].astype(BF16), s_d[gr]) + y0[gr] for gr in groups}
    s_new = {}
    for gr in groups:
        n_h = jnp.zeros((N_B, gw), F32)
        for i in range(4):
            n_h = n_h + jnp.where(lane_head == i, n_full[gr][i * N_B:(i + 1) * N_B, :], 0.0)
        s_new[gr] = (s_old[gr] * g_last[gr]
                     - _dot(s_old[gr].astype(BF16), jnp.where(bd_gg, m_full[gr], 0.0).astype(BF16)) + n_h)
        s_scr[gr[0], gr[1]] = s_new[gr]

    @pl.when(c == pl.num_programs(1) - 1)
    def _():
        for j, g in groups:
            for i in range(4):
                s_out_ref[j, 4 * g + i] = s_new[(j, g)][:, i * N_B:(i + 1) * N_B]

    def head_means(xs):
        pieces = []
        for gr in groups:
            hi = xs[gr].astype(BF16).astype(F32)
            pieces += [hi, xs[gr] - hi]
        tot = _dot(jnp.concatenate(pieces, axis=0).astype(BF16), ones_gg) * (1.0 / N_B)
        return {gr: tot[2 * i * chunk:(2 * i + 1) * chunk] + tot[(2 * i + 1) * chunk:(2 * i + 2) * chunk]
                for i, gr in enumerate(groups)}

    mean = head_means(y)
    dev = {gr: y[gr] - mean[gr] for gr in groups}
    var = head_means({gr: dev[gr] * dev[gr] for gr in groups})
    for j in range(n_sb):
        outs = [dev[(j, g)] * lax.rsqrt(var[(j, g)] + GN_EPS) for g in range(n_g)]
        yn = jnp.concatenate(outs, axis=-1) * lnw_ref[...] + lnb_ref[...]
        o_ref[j] = (yn + bonus_ref[j]) * g_ref[j]


def _rwkv_scan(streams, ln_w, ln_b, s0, chunk, n_sb):
    _, n_seq, t, _ = streams.shape
    n_chunks = t // chunk
    tri = jnp.tril(jnp.ones((chunk, chunk), F32)).astype(BF16)
    row = pl.BlockSpec((n_sb, chunk, D_B), lambda b, c: (b, c, 0))
    st = pl.BlockSpec((n_sb, H_B, N_B, N_B), lambda b, c: (b, 0, 0, 0))
    stream = lambda k: pl.BlockSpec((None, n_sb, chunk, D_B), lambda b, c: (k, b, c, 0))
    return pl.pallas_call(
        functools.partial(_rwkv_scan_kernel, chunk=chunk, n_sb=n_sb),
        grid=(n_seq // n_sb, n_chunks),
        in_specs=[stream(k) for k in range(N_STREAMS)] + [_full(ln_w.shape), _full(ln_b.shape), st, _full(tri.shape)],
        out_specs=[row, st],
        out_shape=[jax.ShapeDtypeStruct((n_seq, t, D_B), F32), jax.ShapeDtypeStruct(s0.shape, F32)],
        scratch_shapes=[pltpu.VMEM((n_sb, H_B // 4, N_B, 4 * N_B), F32)],
        compiler_params=_cparams(("parallel", "arbitrary")),
        name="rwkv_scan",
    )(*([streams] * N_STREAMS), ln_w, ln_b, s0, tri)


def _pool_prompt_kernel(x_ref, gpre_ref, wpool_ref, pscale_ref, gpost_ref, o_ref, tail_ref, hext, *, tm):
    j = pl.program_id(1)
    x = x_ref[...]
    h = _rms(x, gpre_ref[...])

    @pl.when(j == 0)
    def _():
        hext[0:W_MAX, :] = jnp.zeros((W_MAX, D_MODEL), F32)

    @pl.when(j > 0)
    def _():
        hext[0:W_MAX, :] = hext[tm:tm + W_MAX, :]

    hext[W_MAX:, :] = h
    tail_ref[0] = h[tm - W_MAX:, :]
    pos = j * tm + lax.broadcasted_iota(jnp.int32, (tm, 1), 0)
    acc = hext[...]
    ys = []
    for gi, w in enumerate(POOL_WINDOWS):
        acc = acc[:, POOL_GC * (1 if gi else 0):]
        acc = acc + pltpu.roll(acc, w // 2, 0)
        cnt = jnp.minimum(pos + 1, w).astype(F32)
        pooled = acc[W_MAX:, :POOL_GC] / cnt - h[:, gi * POOL_GC:(gi + 1) * POOL_GC]
        ys.append(_dot(pooled.astype(BF16), wpool_ref[gi]))
    y = jnp.concatenate(ys, axis=-1) * pscale_ref[...]
    o_ref[...] = x + _rms(y, gpost_ref[...])


def _pool_prompt(x, g_pre, w_pool, p_scale, g_post, n_b, t_p):
    m = x.shape[0]
    tm = POOL_TILE
    n_t = t_p // tm
    row = pl.BlockSpec((tm, D_MODEL), lambda b, j: (b * n_t + j, 0))
    return pl.pallas_call(
        functools.partial(_pool_prompt_kernel, tm=tm),
        grid=(n_b, n_t),
        in_specs=[row, _full(g_pre.shape), _full(w_pool.shape), _full(p_scale.shape), _full(g_post.shape)],
        out_specs=[row, pl.BlockSpec((1, W_MAX, D_MODEL), lambda b, j: (b, 0, 0))],
        out_shape=[jax.ShapeDtypeStruct((m, D_MODEL), F32), jax.ShapeDtypeStruct((n_b, W_MAX, D_MODEL), F32)],
        scratch_shapes=[pltpu.VMEM((W_MAX + tm, D_MODEL), F32)],
        compiler_params=_cparams(("parallel", "arbitrary")),
        name="pool_prompt",
    )(x, g_pre, w_pool, p_scale, g_post)


def _pool_sample_kernel(x_ref, pre_ref, gpre_ref, wpool_ref, pscale_ref, gpost_ref, o_ref, h_ref, *, n_b, t_s):
    x = x_ref[...]
    h = _rms(x, gpre_ref[...])
    h_ref[...] = h
    n_pre = W_MAX - 1
    rows = [pre_ref[i * n_b:(i + 1) * n_b, :] for i in range(n_pre)] + [h[t * n_b:(t + 1) * n_b, :] for t in range(t_s)]
    outs = []
    for t in range(t_s):
        ys = []
        for gi, w in enumerate(POOL_WINDOWS):
            sl = slice(gi * POOL_GC, (gi + 1) * POOL_GC)
            win = rows[n_pre + t][:, sl]
            for d in range(1, w):
                win = win + rows[n_pre + t - d][:, sl]
            pooled = win / float(w) - rows[n_pre + t][:, sl]
            ys.append(_dot(pooled.astype(BF16), wpool_ref[gi]))
        outs.append(jnp.concatenate(ys, axis=-1))
    y = jnp.concatenate(outs, axis=0) * pscale_ref[...]
    o_ref[...] = x + _rms(y, gpost_ref[...])


def _pool_sample(x, prefix, g_pre, w_pool, p_scale, g_post, n_b, t_s):
    args = (x, prefix, g_pre, w_pool, p_scale, g_post)
    return pl.pallas_call(
        functools.partial(_pool_sample_kernel, n_b=n_b, t_s=t_s),
        grid=(1,),
        in_specs=[_full(a.shape) for a in args],
        out_specs=[_full(x.shape), _full(x.shape)],
        out_shape=[jax.ShapeDtypeStruct(x.shape, F32)] * 2,
        compiler_params=_cparams(("arbitrary",)),
        name="pool_sample",
    )(*args)


def _rope_tables(pos):
    half = ROPE_DIM // 2
    inv = ROPE_BASE ** (-jnp.arange(half, dtype=F32) / half)
    ang = pos[:, None] * inv[None, :]
    cos, sin = jnp.cos(ang), jnp.sin(ang)
    reps = LANES // ROPE_DIM
    return jnp.tile(jnp.concatenate([cos, cos], -1), (1, reps)), jnp.tile(jnp.concatenate([-sin, sin], -1), (1, reps))


def _swap_halves(w):
    half = ROPE_DIM // 2
    return jnp.concatenate([w[..., half:], w[..., :half]], axis=-1)


def _ab_weights(e, w_in, g_q, w_uq, g_kv, w_uk):
    w = w_in[e]
    w_pe = w[:, Q_LORA + KV_LORA:MLA_IN]
    wq2 = jnp.concatenate([w[:, :Q_LORA], w_pe, _swap_halves(w_pe)], axis=1).astype(BF16)
    wkv = w[:, Q_LORA:Q_LORA + KV_LORA].astype(BF16)
    wb = w[:, MLA_IN:].astype(BF16)
    uq = w_uq[e].reshape(Q_LORA, H_A, NOPE_DIM + ROPE_DIM)
    uq_pe = uq[:, :, NOPE_DIM:]
    wuq = jnp.concatenate([uq[:, :, :NOPE_DIM].reshape(Q_LORA, H_A * NOPE_DIM),
                           uq_pe.reshape(Q_LORA, H_A * ROPE_DIM),
                           _swap_halves(uq_pe).reshape(Q_LORA, H_A * ROPE_DIM)], axis=1).astype(BF16)
    wuk = jnp.transpose(w_uk[e], (1, 2, 0)).astype(BF16)
    src = jnp.arange(H_A * ROPE_DIM)
    dst = (src // ROPE_DIM) * LANES + src % ROPE_DIM
    spread = (dst[:, None] == jnp.arange(H_A * LANES)[None, :]).astype(BF16)
    return wq2, wkv, wb, g_q[e][None], g_kv[e][None], wuq, wuk, spread


def _rwkv_weights(e, mu_shift, w0, w2, a0, a2, g2, k_k, k_a, r_k):
    head = jnp.arange(D_B) // N_B
    ones = (head[:, None] == head[None, :]).astype(BF16)
    return (mu_shift[e][None], w0[e][None], w2[e], a0[e][None], a2[e], g2[e],
            k_k[e][None], k_a[e][None], r_k[e].reshape(1, D_B), ones)


def kernel(x_prompt, x_sample, cache_ckv, cache_kpe, page_table, state_wkv, state_shift, state_pool, meta_tokens,
           g_mix_pre, g_mix_post, g_ffn_pre, g_ffn_post, w_in, g_q, w_uq, g_kv, w_uk, w_uv, mu_shift, w0, w2, a0, a2,
           g2, k_k, k_a, r_k, ln_w, ln_b, w_out, w_pool, pool_scale, w_up, w_down):
    n_bp, seq, _ = x_prompt.shape
    n_bs, t_s, _ = x_sample.shape
    depth = g_mix_pre.shape[0]
    t_p = seq + N_META
    n_pages = page_table.shape[1]
    past = n_pages * PAGE_SIZE
    m_p = n_bp * t_p
    m_s = n_bs * t_s

    meta = jnp.broadcast_to(meta_tokens[None].astype(x_prompt.dtype), (n_bp, N_META, D_MODEL))
    xp = jnp.concatenate([meta, x_prompt], axis=1).reshape(m_p, D_MODEL)
    xs = jnp.transpose(x_sample, (1, 0, 2)).reshape(m_s, D_MODEL)

    tm_s = m_s
    cos_p, sin_p = _rope_tables((jnp.arange(t_p + ROW_TILE) % t_p).astype(F32))
    cos_s, sin_s = _rope_tables((past + (jnp.arange(m_s + tm_s) % m_s) // n_bs).astype(F32))
    cache_kpe_t = jnp.swapaxes(cache_kpe, 2, 3)

    ckv_p, kpe_p, wkv_p, shift_p, pool_p = [], [], [], [], []
    ckv_s, kpe_s, wkv_s, shift_s, pool_s = [], [], [], [], []
    for l in range(depth):
        g_pre, g_post = g_mix_pre[l][None], g_mix_post[l][None]
        if l % 2 == 0:
            e = l // 2
            ab_w = _ab_weights(e, w_in, g_q, w_uq, g_kv, w_uk)
            rw = _rwkv_weights(e, mu_shift, w0, w2, a0, a2, g2, k_k, k_a, r_k)
            wuv = jnp.transpose(w_uv[e], (1, 0, 2)).astype(BF16)
            wout = w_out[e].astype(BF16)
            lnw, lnb = ln_w[e][None], ln_b[e][None]

            ckv, kpe, zb, qabs, kvb = _ab_in_proj(xp, g_pre, ab_w, cos_p, sin_p, ROW_TILE)
            olat = _attn_prompt(qabs, kvb, n_bp, t_p)
            zb3 = zb.reshape(n_bp, t_p, RWKV_IN)
            first = jnp.zeros((n_bp, 8, RWKV_IN), F32)
            streams = _rwkv_prep(zb, first, rw, PREP_TILE, 1).reshape(N_STREAMS, n_bp, t_p, D_B)
            o_b, s_fin = _rwkv_scan(streams, lnw, lnb, jnp.zeros((n_bp, H_B, N_B, N_B), F32), SCAN_CHUNK_P,
                                    SCAN_SEQS_P)
            xp = _ab_out_proj(olat, o_b.reshape(m_p, D_B), xp, wuv, wout, g_post, ROW_TILE)
            ckv_p.append(ckv.reshape(n_bp, t_p, KV_LORA))
            kpe_p.append(kpe.reshape(n_bp, t_p, ROPE_DIM))
            wkv_p.append(s_fin)
            shift_p.append(zb3[:, -1])

            ckv, kpe, zb, qabs, kvb = _ab_in_proj(xs, g_pre, ab_w, cos_s, sin_s, tm_s)
            q_s = jnp.transpose(qabs.reshape(H_A, t_s, n_bs, QK_PAD), (2, 0, 1, 3)).reshape(n_bs, H_A * t_s, QK_PAD)
            k_new = jnp.pad(jnp.transpose(kvb.reshape(t_s, n_bs, QK_PAD), (1, 0, 2)),
                            ((0, 0), (0, NEW_KEY_ROWS - t_s), (0, 0)))
            o_s = _attn_sample(page_table, q_s, k_new, cache_ckv, cache_kpe_t, e, t_s)
            olat = jnp.transpose(o_s.reshape(n_bs, H_A, t_s, KV_LORA), (1, 2, 0, 3)).reshape(H_A, m_s, KV_LORA)
            streams = _rwkv_prep(zb, state_shift[e].astype(F32)[None], rw, tm_s, n_bs)
            streams = jnp.pad(jnp.transpose(streams.reshape(N_STREAMS, t_s, n_bs, D_B), (0, 2, 1, 3)),
                              ((0, 0), (0, 0), (0, SCAN_CHUNK_S - t_s), (0, 0)))
            o_b, s_fin = _rwkv_scan(streams, lnw, lnb, state_wkv[e].astype(F32), SCAN_CHUNK_S, SCAN_SEQS_S)
            o_b = jnp.transpose(o_b[:, :t_s], (1, 0, 2)).reshape(m_s, D_B)
            xs = _ab_out_proj(olat, o_b, xs, wuv, wout, g_post, tm_s)
            ckv_s.append(jnp.transpose(ckv.reshape(t_s, n_bs, KV_LORA), (1, 0, 2)))
            kpe_s.append(jnp.transpose(kpe.reshape(t_s, n_bs, ROPE_DIM), (1, 0, 2)))
            wkv_s.append(s_fin)
            shift_s.append(zb[m_s - n_bs:])
        else:
            o = l // 2
            wp = w_pool[o].astype(BF16)
            ps = pool_scale[o][None]
            xp, tail = _pool_prompt(xp, g_pre, wp, ps, g_post, n_bp, t_p)
            pool_p.append(tail[:, 1:])
            prefix = jnp.transpose(state_pool[o].astype(F32), (1, 0, 2))
            xs, h_s = _pool_sample(xs, prefix.reshape((W_MAX - 1) * n_bs, D_MODEL), g_pre, wp, ps, g_post, n_bs, t_s)
            full = jnp.concatenate([prefix, h_s.reshape(t_s, n_bs, D_MODEL)], axis=0)
            pool_s.append(jnp.transpose(full[-(W_MAX - 1):], (1, 0, 2)))
        gfp, gfo = g_ffn_pre[l][None], g_ffn_post[l][None]
        xp, xs = _ffn(xp, xs, gfp, w_up, w_down, gfo, l, ROW_TILE)

    y_prompt = xp.reshape(n_bp, t_p, D_MODEL)[:, N_META:]
    y_sample = jnp.transpose(xs.reshape(t_s, n_bs, D_MODEL), (1, 0, 2))
    return (y_prompt, y_sample, jnp.stack(ckv_p), jnp.stack(kpe_p), jnp.stack(wkv_p), jnp.stack(shift_p),
            jnp.stack(pool_p), jnp.stack(ckv_s), jnp.stack(kpe_s), jnp.stack(wkv_s), jnp.stack(shift_s),
            jnp.stack(pool_s))
```

```python
import functools

import jax
import jax.numpy as jnp
from jax import lax
from jax.experimental import pallas as pl
from jax.experimental.pallas import tpu as pltpu

F32 = jnp.float32
BF16 = jnp.bfloat16

D_MODEL = 1024
N_META = 16
PAGE_SIZE = 128
H_A = 8
Q_LORA = 384
KV_LORA = 256
NOPE_DIM = 64
ROPE_DIM = 32
V_DIM = 64
ROPE_BASE = 10000.0
SM_SCALE = (NOPE_DIM + ROPE_DIM) ** -0.5
LOG2_E = 1.4426950408889634
H_B = 8
N_B = 64
D_B = H_B * N_B
DECAY_LORA = 64
AAA_LORA = 64
GATE_LORA = 160
RWKV_IN = 3 * D_B + DECAY_LORA + AAA_LORA + GATE_LORA
MLA_IN = Q_LORA + KV_LORA + ROPE_DIM
GN_EPS = 64e-5
POOL_WINDOWS = (2, 4, 8, 16)
POOL_GC = D_MODEL // len(POOL_WINDOWS)
W_MAX = 16
D_FF = 4 * D_MODEL
RMS_EPS = 1e-6

LANES = 128
QK_PAD = KV_LORA + LANES
VMEM_LIMIT = 56 * 1024 * 1024
ROW_TILE = 384
ATT_Q_TILE = 256
ATT_SCORE_LAG = 2
POOL_TILE = 688
PREP_TILE = 344
SCAN_CHUNK_P = 48
SCAN_CHUNK_S = 8
SCAN_SEQS_P = 8
SCAN_SEQS_S = 16
NEW_KEY_ROWS = 16
N_STREAMS = 8
PAGES_PER_CHUNK = 8
FFN_STAGE_BYTES = 2 * 1024 * 1024


def _cparams(sem):
    return pltpu.CompilerParams(dimension_semantics=sem, vmem_limit_bytes=VMEM_LIMIT)


def _rms(x, g):
    return x * lax.rsqrt(jnp.mean(x * x, axis=-1, keepdims=True) + RMS_EPS) * g


def _dot(a, b):
    return jnp.dot(a, b, preferred_element_type=F32)


def _dot_nt(a, b):
    return lax.dot_general(a, b, (((1,), (1,)), ((), ())), preferred_element_type=F32)


def _dot_tn(a, b):
    return lax.dot_general(a, b, (((0,), (0,)), ((), ())), preferred_element_type=F32)


def _full(shape):
    n = len(shape)
    return pl.BlockSpec(shape, lambda *_: (0,) * n)


def _ab_in_kernel(x_ref, gpre_ref, wq2_ref, wkv_ref, wb_ref, gq_ref, gkv_ref, wuq_ref, wuk_ref, spread_ref, cos_ref, sin_ref,
                  ckv_ref, kpe_ref, zb_ref, qabs_ref, kvb_ref, *, period):
    h = _rms(x_ref[...], gpre_ref[...]).astype(BF16)
    zq2 = _dot(h, wq2_ref[...])
    zkv = _dot(h, wkv_ref[...])
    zb_ref[...] = _dot(h, wb_ref[...])
    ckv = _rms(zkv, gkv_ref[...])
    ckv_ref[...] = ckv
    tm = x_ref.shape[0]
    start = pl.multiple_of((pl.program_id(0) * tm) % period, 8)
    cos = cos_ref[pl.ds(start, tm), :]
    sin = sin_ref[pl.ds(start, tm), :]
    pe = zq2[:, Q_LORA:Q_LORA + ROPE_DIM]
    pe_sw = zq2[:, Q_LORA + ROPE_DIM:Q_LORA + 2 * ROPE_DIM]
    kpe = pe * cos[:, :ROPE_DIM] + pe_sw * sin[:, :ROPE_DIM]
    kpe_ref[...] = kpe
    kvb_ref[:, :KV_LORA] = ckv.astype(BF16)
    kvb_ref[:, KV_LORA:] = jnp.concatenate([kpe, jnp.zeros((kpe.shape[0], LANES - ROPE_DIM), F32)], axis=1).astype(BF16)
    qn = _rms(zq2[:, :Q_LORA], gq_ref[...]).astype(BF16)
    q2 = _dot(qn, wuq_ref[...])
    n_nope = H_A * NOPE_DIM
    n_pe = H_A * ROPE_DIM
    reps = n_pe // LANES
    q_pe = (q2[:, n_nope:n_nope + n_pe] * jnp.concatenate([cos] * reps, axis=1)
            + q2[:, n_nope + n_pe:] * jnp.concatenate([sin] * reps, axis=1)).astype(BF16)
    slabs = _dot(q_pe, spread_ref[...]).astype(BF16)
    for hd in range(H_A):
        qn_h = q2[:, hd * NOPE_DIM:(hd + 1) * NOPE_DIM].astype(BF16)
        qabs_ref[hd, :, 0:KV_LORA] = _dot(qn_h, wuk_ref[hd]).astype(BF16)
        qabs_ref[hd, :, KV_LORA:QK_PAD] = slabs[:, hd * LANES:(hd + 1) * LANES]


def _ab_in_proj(x, g_pre, wts, cos_t, sin_t, tm):
    m = x.shape[0]
    wq2, wkv, wb, g_q, g_kv, wuq, wuk, spread = wts
    row = lambda w: pl.BlockSpec((tm, w), lambda i: (i, 0))
    return pl.pallas_call(
        functools.partial(_ab_in_kernel, period=cos_t.shape[0] - tm),
        grid=(m // tm,),
        in_specs=[row(D_MODEL), _full(g_pre.shape), _full(wq2.shape), _full(wkv.shape), _full(wb.shape),
                  _full(g_q.shape), _full(g_kv.shape), _full(wuq.shape), _full(wuk.shape), _full(spread.shape),
                  _full(cos_t.shape),
                  _full(sin_t.shape)],
        out_specs=[row(KV_LORA), row(ROPE_DIM), row(RWKV_IN),
                   pl.BlockSpec((H_A, tm, QK_PAD), lambda i: (0, i, 0)), row(QK_PAD)],
        out_shape=[jax.ShapeDtypeStruct((m, KV_LORA), F32), jax.ShapeDtypeStruct((m, ROPE_DIM), F32),
                   jax.ShapeDtypeStruct((m, RWKV_IN), F32), jax.ShapeDtypeStruct((H_A, m, QK_PAD), BF16),
                   jax.ShapeDtypeStruct((m, QK_PAD), BF16)],
        compiler_params=_cparams(("parallel",)),
        name="ab_in_proj",
    )(x, g_pre, wq2, wkv, wb, g_q, g_kv, wuq, wuk, spread, cos_t, sin_t)


def _attn_prompt_kernel(q_ref, kv_ref, o_ref, kpad, *, t_p, tq):
    t_pad = kpad.shape[0]

    @pl.when(pl.program_id(1) == 0)
    def _():
        kpad[:t_p, :] = kv_ref[0]
        kpad[t_p:, :] = jnp.zeros((t_pad - t_p, QK_PAD), BF16)

    n_t = t_p // tq
    tiles = [(i * tq, t_p - i * tq if i == n_t - 1 else tq) for i in range(n_t)]
    kext = [min(-(-(q0 + rows) // LANES) * LANES, t_pad) for q0, rows in tiles]
    scores = {}
    lag = ATT_SCORE_LAG
    for i in range(n_t + lag):
        if i < n_t:
            q0, rows = tiles[i]
            scores[i] = _dot_nt(q_ref[0, q0:q0 + rows, :], kpad[:kext[i], :])
        if i >= lag:
            q0, rows = tiles[i - lag]
            ke = kext[i - lag]
            s = scores.pop(i - lag)
            lo = (q0 // LANES) * LANES
            q_pos = q0 + lax.broadcasted_iota(jnp.int32, (rows, ke - lo), 0)
            k_pos = lo + lax.broadcasted_iota(jnp.int32, (rows, ke - lo), 1)
            s_diag = jnp.where(k_pos <= q_pos, s[:, lo:], -jnp.inf)
            s = jnp.concatenate([s[:, :lo], s_diag], axis=1) if lo else s_diag
            p = jnp.exp2((s - jnp.max(s, axis=-1, keepdims=True)) * (SM_SCALE * LOG2_E))
            l = jnp.sum(p, axis=-1, keepdims=True)
            o = _dot(p.astype(BF16), kpad[:ke, :KV_LORA])
            o_ref[0, q0:q0 + rows, :] = (o / l).astype(BF16)


def _attn_prompt(qabs, kvb, n_b, t_p):
    m = qabs.shape[1]
    t_pad = -(-t_p // LANES) * LANES
    return pl.pallas_call(
        functools.partial(_attn_prompt_kernel, t_p=t_p, tq=ATT_Q_TILE),
        grid=(n_b, H_A),
        in_specs=[pl.BlockSpec((1, t_p, QK_PAD), lambda b, h: (h, b, 0)),
                  pl.BlockSpec((1, t_p, QK_PAD), lambda b, h: (b, 0, 0))],
        out_specs=pl.BlockSpec((1, t_p, KV_LORA), lambda b, h: (h, b, 0)),
        out_shape=jax.ShapeDtypeStruct((H_A, m, KV_LORA), BF16),
        scratch_shapes=[pltpu.VMEM((t_pad, QK_PAD), BF16)],
        compiler_params=_cparams(("parallel", "arbitrary")),
        name="attn_prompt",
    )(qabs, kvb.reshape(n_b, t_p, QK_PAD))


def _attn_sample_kernel(pt_ref, q_ref, knew_ref, ckv_hbm, kpe_hbm, o_ref, ckv_buf, kpe_buf, sems, *, layer, n_pages,
                        t_s):
    b = pl.program_id(0)
    n_b = pl.num_programs(0)
    slot = b % 2

    def page_copies(bb, sl, p):
        page = pt_ref[bb, p]
        return (pltpu.make_async_copy(ckv_hbm.at[layer, page], ckv_buf.at[sl, p], sems.at[0, sl]),
                pltpu.make_async_copy(kpe_hbm.at[layer, page], kpe_buf.at[sl, p], sems.at[1, sl]))

    def start_all(bb, sl):
        for p in range(n_pages):
            latent_cp, rope_cp = page_copies(bb, sl, p)
            latent_cp.start()
            rope_cp.start(priority=1)

    @pl.when(b == 0)
    def _():
        start_all(0, 0)

    @pl.when(b + 1 < n_b)
    def _():
        start_all(b + 1, 1 - slot)

    for p in range(n_pages):
        for cp in page_copies(b, slot, p):
            cp.wait()

    q = q_ref[0]
    q_lat = q[:, :KV_LORA]
    q_pe = q[:, KV_LORA:KV_LORA + ROPE_DIM]
    rows = PAGES_PER_CHUNK * PAGE_SIZE
    n_chunks = n_pages // PAGES_PER_CHUNK

    def latent_rows(c):
        return ckv_buf[slot, c * PAGES_PER_CHUNK:(c + 1) * PAGES_PER_CHUNK].reshape(rows, KV_LORA).astype(BF16)

    parts = []
    scores = {}
    lag = ATT_SCORE_LAG
    for c in range(n_chunks + lag):
        if c < n_chunks:
            kp = jnp.concatenate([kpe_buf[slot, c * PAGES_PER_CHUNK + i] for i in range(PAGES_PER_CHUNK)], axis=1)
            scores[c] = (_dot_nt(q_lat, latent_rows(c)) + _dot(q_pe, kp.astype(BF16))) * SM_SCALE
        if c >= lag:
            s = scores.pop(c - lag)
            m_c = jnp.max(s, axis=-1, keepdims=True)
            p = jnp.exp(s - m_c)
            parts.append((m_c, jnp.sum(p, axis=-1, keepdims=True), _dot(p.astype(BF16), latent_rows(c - lag))))
    kn = knew_ref[0]
    s_new = _dot_nt(q, kn) * SM_SCALE
    step = lax.broadcasted_iota(jnp.int32, s_new.shape, 0) % t_s
    key = lax.broadcasted_iota(jnp.int32, s_new.shape, 1)
    s_new = jnp.where(key <= step, s_new, -jnp.inf)
    m_n = jnp.max(s_new, axis=-1, keepdims=True)
    p_new = jnp.exp(s_new - m_n)
    parts.append((m_n, jnp.sum(p_new, axis=-1, keepdims=True), _dot(p_new.astype(BF16), kn[:, :KV_LORA])))
    m = parts[0][0]
    for m_c, _, _ in parts[1:]:
        m = jnp.maximum(m, m_c)
    l = jnp.zeros_like(m)
    acc = jnp.zeros((q.shape[0], KV_LORA), F32)
    for m_c, l_c, acc_c in parts:
        w_c = jnp.exp(m_c - m)
        l = l + w_c * l_c
        acc = acc + w_c * acc_c
    o_ref[0] = (acc / l).astype(BF16)


def _attn_sample(page_table, q_s, k_new, cache_ckv, cache_kpe, layer, t_s):
    n_b, n_pages = page_table.shape
    n_q = q_s.shape[1]
    grid_spec = pltpu.PrefetchScalarGridSpec(
        num_scalar_prefetch=1,
        grid=(n_b,),
        in_specs=[pl.BlockSpec((1, n_q, QK_PAD), lambda b, pt: (b, 0, 0)),
                  pl.BlockSpec((1,) + k_new.shape[1:], lambda b, pt: (b, 0, 0)),
                  pl.BlockSpec(memory_space=pl.ANY),
                  pl.BlockSpec(memory_space=pl.ANY)],
        out_specs=pl.BlockSpec((1, n_q, KV_LORA), lambda b, pt: (b, 0, 0)),
        scratch_shapes=[pltpu.VMEM((2, n_pages, PAGE_SIZE, KV_LORA), F32),
                        pltpu.VMEM((2, n_pages, ROPE_DIM, PAGE_SIZE), F32),
                        pltpu.SemaphoreType.DMA((2, 2))],
    )
    return pl.pallas_call(
        functools.partial(_attn_sample_kernel, layer=layer, n_pages=n_pages, t_s=t_s),
        grid_spec=grid_spec,
        out_shape=jax.ShapeDtypeStruct((n_b, n_q, KV_LORA), BF16),
        compiler_params=_cparams(("arbitrary",)),
        name="attn_sample",
    )(page_table, q_s, k_new, cache_ckv, cache_kpe)


def _ab_out_kernel(olat_ref, ob_ref, x_ref, wuv_ref, wout_ref, gpost_ref, o_ref, cat_ref):
    n_a = H_A * V_DIM
    for hd in range(0, H_A, 2):
        pair = [_dot(olat_ref[hd + i], wuv_ref[hd + i]) for i in range(2)]
        cat_ref[:, hd * V_DIM:(hd + 2) * V_DIM] = jnp.concatenate(pair, axis=1).astype(BF16)
    cat_ref[:, n_a:] = ob_ref[...].astype(BF16)
    y = _dot(cat_ref[...], wout_ref[...])
    o_ref[...] = x_ref[...] + _rms(y, gpost_ref[...])


def _ab_out_proj(olat, o_b, x, wuv, wout, g_post, tm):
    m = x.shape[0]
    row = lambda w: pl.BlockSpec((tm, w), lambda i: (i, 0))
    return pl.pallas_call(
        _ab_out_kernel,
        grid=(m // tm,),
        in_specs=[pl.BlockSpec((H_A, tm, KV_LORA), lambda i: (0, i, 0)), row(D_B), row(D_MODEL),
                  _full(wuv.shape), _full(wout.shape), _full(g_post.shape)],
        out_specs=row(D_MODEL),
        out_shape=jax.ShapeDtypeStruct((m, D_MODEL), F32),
        scratch_shapes=[pltpu.VMEM((tm, H_A * V_DIM + D_B), BF16)],
        compiler_params=_cparams(("parallel",)),
        name="ab_out_proj",
    )(olat, o_b, x, wuv, wout, g_post)


def _ffn_kernel(xp_ref, xs_ref, gpre_ref, wup_hbm, wdown_hbm, gpost_ref, op_ref, os_ref, wup_ref, wdown_ref, st_up,
                st_dn, sems, *, ff_chunk, layer, n_p):
    @pl.when(pl.program_id(0) == 0)
    def _():
        jobs = []
        for k, (src, dst, st) in enumerate(((wup_hbm, wup_ref, st_up), (wdown_hbm, wdown_ref, st_dn))):
            rows = st.shape[1]
            for c in range(dst.shape[0] // rows):
                jobs.append((k, c, rows, src, dst, st))

        def copy(job, slot):
            k, c, rows, src, _, st = job
            return pltpu.make_async_copy(src.at[layer, c * rows:(c + 1) * rows], st.at[slot], sems.at[k, slot])

        copy(jobs[0], 0).start()
        for i, job in enumerate(jobs):
            slot = i % 2
            if i + 1 < len(jobs):
                copy(jobs[i + 1], 1 - slot).start()
            copy(job, slot).wait()
            k, c, rows, _, dst, st = job
            dst[c * rows:(c + 1) * rows, :] = st[slot].astype(BF16)

    def mlp(x_ref, o_ref):
        x = x_ref[...]
        h = _rms(x, gpre_ref[...]).astype(BF16)
        y = jnp.zeros(x.shape, F32)
        for c in range(D_FF // ff_chunk):
            u = _dot(h, wup_ref[:, c * ff_chunk:(c + 1) * ff_chunk])
            a = jnp.square(jnp.maximum(u, 0.0)).astype(BF16)
            y = y + _dot(a, wdown_ref[c * ff_chunk:(c + 1) * ff_chunk, :])
        o_ref[...] = x + _rms(y, gpost_ref[...])

    @pl.when(pl.program_id(0) < n_p)
    def _():
        mlp(xp_ref, op_ref)

    @pl.when(pl.program_id(0) == n_p)
    def _():
        mlp(xs_ref, os_ref)


def _ffn(xp, xs, g_pre, w_up, w_down, g_post, layer, tm):
    n_p = xp.shape[0] // tm
    row = pl.BlockSpec((tm, D_MODEL), lambda i: (jnp.minimum(i, n_p - 1), 0))
    return pl.pallas_call(
        functools.partial(_ffn_kernel, ff_chunk=1024, layer=layer, n_p=n_p),
        grid=(n_p + 1,),
        in_specs=[row, _full(xs.shape), _full(g_pre.shape), pl.BlockSpec(memory_space=pl.ANY),
                  pl.BlockSpec(memory_space=pl.ANY), _full(g_post.shape)],
        out_specs=[row, _full(xs.shape)],
        out_shape=[jax.ShapeDtypeStruct(xp.shape, F32), jax.ShapeDtypeStruct(xs.shape, F32)],
        scratch_shapes=[pltpu.VMEM(w_up.shape[1:], BF16), pltpu.VMEM(w_down.shape[1:], BF16),
                        pltpu.VMEM((2, FFN_STAGE_BYTES // (4 * D_FF), D_FF), F32),
                        pltpu.VMEM((2, FFN_STAGE_BYTES // (4 * D_MODEL), D_MODEL), F32),
                        pltpu.SemaphoreType.DMA((2, 2))],
        compiler_params=_cparams(("arbitrary",)),
        name="ffn",
    )(xp, xs, g_pre, w_up, w_down, g_post)


def _split_dot(x, w):
    hi = x.astype(BF16)
    lo = (x - hi.astype(F32)).astype(BF16)
    return _dot(hi, w) + _dot(lo, w)


def _rwkv_prep_kernel(zb_ref, first_ref, mu_ref, w0_ref, w2_ref, a0_ref, a2_ref, g2_ref, kk_ref, ka_ref, rk_ref,
                      ones_ref, out_ref, carry, *, shift):
    @pl.when(pl.program_id(1) == 0)
    def _():
        carry[...] = first_ref[0]

    zb = zb_ref[...]
    tm = zb.shape[0]
    if shift == 1:
        row = lax.broadcasted_iota(jnp.int32, (tm, 1), 0)
        prev = jnp.where(row == 0, carry[0:1, :], pltpu.roll(zb, 1, 0))
        carry[0:1, :] = zb[tm - 1:tm, :]
    else:
        prev = jnp.concatenate([carry[...], zb[:tm - shift]], axis=0)
        carry[...] = zb[tm - shift:]
    xm = zb + (prev - zb) * mu_ref[...]
    o3 = 3 * D_B
    o4 = o3 + DECAY_LORA
    o5 = o4 + AAA_LORA
    r = xm[:, :D_B]
    k = xm[:, D_B:2 * D_B]
    v = xm[:, 2 * D_B:o3]
    z = -(w0_ref[...] + _dot(jnp.tanh(xm[:, o3:o4]).astype(BF16), w2_ref[...].astype(BF16)))
    softplus = jnp.maximum(z, 0.0) + jnp.log(1.0 + jnp.exp(-jnp.abs(z)))
    out_ref[1] = -jnp.exp(-softplus - 0.5)
    a = 1.0 / (1.0 + jnp.exp(-(a0_ref[...] + _dot(xm[:, o4:o5].astype(BF16), a2_ref[...].astype(BF16)))))
    sg = 1.0 / (1.0 + jnp.exp(-xm[:, o5:]))
    out_ref[6] = _dot(sg.astype(BF16), g2_ref[...].astype(BF16))
    ones = ones_ref[...]
    kk = k * kk_ref[...]
    kk = kk / jnp.maximum(jnp.sqrt(_split_dot(kk * kk, ones)), 1e-12)
    keff = k * (1.0 + (a - 1.0) * ka_ref[...])
    out_ref[0] = r
    out_ref[2] = keff
    out_ref[3] = v
    out_ref[4] = kk
    out_ref[5] = kk * a
    out_ref[7] = _split_dot(r * keff * rk_ref[...], ones) * v


def _rwkv_prep(zb, first, rw, tm, shift):
    m = zb.shape[0]
    n_seq = first.shape[0]
    n_t = m // (n_seq * tm)
    row = lambda w: pl.BlockSpec((tm, w), lambda b, j: (b * n_t + j, 0))
    return pl.pallas_call(
        functools.partial(_rwkv_prep_kernel, shift=shift),
        grid=(n_seq, n_t),
        in_specs=[row(RWKV_IN), pl.BlockSpec((1,) + first.shape[1:], lambda b, j: (b, 0, 0))]
        + [_full(w.shape) for w in rw],
        out_specs=pl.BlockSpec((N_STREAMS, tm, D_B), lambda b, j: (0, b * n_t + j, 0)),
        out_shape=jax.ShapeDtypeStruct((N_STREAMS, m, D_B), F32),
        scratch_shapes=[pltpu.VMEM(first.shape[1:], F32)],
        compiler_params=_cparams(("parallel", "arbitrary")),
        name="rwkv_prep",
    )(zb, first, *rw)


def _rwkv_scan_kernel(r_ref, lw_ref, keff_ref, v_ref, kn_ref, b_ref, g_ref, bonus_ref, lnw_ref, lnb_ref, s0_ref,
                      tri_ref, o_ref, s_out_ref, s_scr, *, chunk, n_sb):
    c = pl.program_id(1)
    n_g = H_B // 4
    gw = 4 * N_B
    hw = 4 * chunk
    groups = [(j, g) for j in range(n_sb) for g in range(n_g)]

    @pl.when(c == 0)
    def _():
        for j, g in groups:
            s_scr[j, g] = jnp.concatenate([s0_ref[j, 4 * g + i] for i in range(4)], axis=1)

    def blk(idx, size):
        return ((idx >= size).astype(jnp.int32) + (idx >= 2 * size).astype(jnp.int32)
                + (idx >= 3 * size).astype(jnp.int32))

    t_i = lax.broadcasted_iota(jnp.int32, (chunk, hw), 0)
    col = lax.broadcasted_iota(jnp.int32, (chunk, hw), 1)
    s_i = col - chunk * blk(col, chunk)
    strict = s_i < t_i
    incl = s_i <= t_i
    eye = jnp.where(s_i == t_i, 1.0, 0.0).astype(F32)
    levels = []
    k_lvl = 1
    while (1 << (k_lvl - 1)) < chunk:
        same_new = (t_i >> k_lvl) == (s_i >> k_lvl)
        same_old = (t_i >> (k_lvl - 1)) == (s_i >> (k_lvl - 1))
        levels.append(strict & same_new & jnp.logical_not(same_old))
        k_lvl += 1
    bd_hh = (blk(lax.broadcasted_iota(jnp.int32, (hw, hw), 0), chunk)
             == blk(lax.broadcasted_iota(jnp.int32, (hw, hw), 1), chunk))
    bd_hg = (blk(lax.broadcasted_iota(jnp.int32, (hw, gw), 0), chunk)
             == (lax.broadcasted_iota(jnp.int32, (hw, gw), 1) >> 6))
    lane_head = lax.broadcasted_iota(jnp.int32, (N_B, gw), 1) >> 6
    bd_gg = (lax.broadcasted_iota(jnp.int32, (gw, gw), 0) >> 6) == (lax.broadcasted_iota(jnp.int32, (gw, gw), 1) >> 6)
    ones_gg = jnp.where(bd_gg, 1.0, 0.0).astype(BF16)

    def bdiag(x, mask):
        return jnp.where(mask, jnp.concatenate([x] * 4, axis=0), 0.0).astype(BF16)

    tri = tri_ref[...]
    qr, bk_d, kk_d, v_d, kq, rq, bke, tkv, g_last = {}, {}, {}, {}, {}, {}, {}, {}, {}
    for j in range(n_sb):
        lw = lw_ref[j]
        hi = lw.astype(BF16)
        r1 = lw - hi.astype(F32)
        mid = r1.astype(BF16)
        lo = (r1 - mid.astype(F32)).astype(BF16)
        cs = _dot(tri, hi) + _dot(tri, mid) + _dot(tri, lo)
        cs_last = cs[chunk - 1:chunk, :]
        g_inv = jnp.exp(-cs)
        g_end = jnp.exp(cs_last - cs)
        kq_j = kn_ref[j] * jnp.exp(cs - lw)
        rq_j = r_ref[j] * jnp.exp(cs)
        b_j = b_ref[j]
        ke_j = keff_ref[j]
        v_j = v_ref[j]
        bk_j = b_j * g_inv
        kk_j = ke_j * g_inv
        bke_j = b_j * g_end
        kke_j = ke_j * g_end
        gl_j = jnp.exp(cs_last)
        for g in range(n_g):
            sl = slice(g * gw, (g + 1) * gw)
            gr = (j, g)
            kq[gr], rq[gr], g_last[gr] = kq_j[:, sl], rq_j[:, sl], gl_j[:, sl]
            qr[gr] = jnp.concatenate([kq[gr], rq[gr]], axis=0).astype(BF16)
            bk_d[gr], kk_d[gr], v_d[gr] = bdiag(bk_j[:, sl], bd_hg), bdiag(kk_j[:, sl], bd_hg), bdiag(v_j[:, sl], bd_hg)
            bke[gr] = bke_j[:, sl]
            tkv[gr] = (v_j[:, sl], kke_j[:, sl])

    p1 = {gr: _dot_nt(qr[gr], bk_d[gr]) for gr in groups}
    p2 = {gr: _dot_nt(qr[gr], kk_d[gr]) for gr in groups}
    a_m = {gr: -p1[gr][:chunk] for gr in groups}
    t_m = {gr: eye + jnp.where(levels[0], a_m[gr], 0.0) for gr in groups}
    w_m = {gr: _dot(jnp.where(strict, p2[gr][:chunk], 0.0).astype(BF16), v_d[gr]) for gr in groups}
    for lvl in levels[1:]:
        x_m = {gr: _dot(jnp.where(lvl, a_m[gr], 0.0).astype(BF16), bdiag(t_m[gr], bd_hh)) for gr in groups}
        t_m = {gr: t_m[gr] + _dot(t_m[gr].astype(BF16), bdiag(x_m[gr], bd_hh)) for gr in groups}
    t_b = {gr: t_m[gr].astype(BF16) for gr in groups}
    kq2 = {gr: _dot(t_b[gr], bdiag(kq[gr], bd_hg)) for gr in groups}
    w2 = {gr: _dot(t_b[gr], bdiag(w_m[gr], bd_hg)) for gr in groups}
    ar = {gr: jnp.where(incl, -p1[gr][chunk:], 0.0).astype(BF16) for gr in groups}
    br = {gr: jnp.where(incl, p2[gr][chunk:], 0.0).astype(BF16) for gr in groups}
    rq2 = {gr: rq[gr] + _dot(ar[gr], bdiag(kq2[gr], bd_hg)) for gr in groups}
    y0 = {gr: _dot(ar[gr], bdiag(w2[gr], bd_hg)) + _dot(br[gr], v_d[gr]) for gr in groups}
    bke_b = {gr: bke[gr].astype(BF16) for gr in groups}
    m_full = {gr: _dot_tn(kq2[gr].astype(BF16), bke_b[gr]) for gr in groups}
    n_full = {gr: _dot_tn(jnp.concatenate([tkv[gr][0], -w2[gr]], axis=0).astype(BF16),
                          jnp.concatenate([tkv[gr][1], bke[gr]], axis=0).astype(BF16)) for gr in groups}

    s_old = {gr: s_scr[gr[0], gr[1]] for gr in groups}
    s_d = {gr: bdiag(s_old[gr], bd_gg) for gr in groups}
    y = {gr: _dot_nt(rq2[gr].astype(BF16), s_d[gr]) + y0[gr] for gr in groups}
    s_new = {}
    for gr in groups:
        n_h = jnp.zeros((N_B, gw), F32)
        for i in range(4):
            n_h = n_h + jnp.where(lane_head == i, n_full[gr][i * N_B:(i + 1) * N_B, :], 0.0)
        s_new[gr] = (s_old[gr] * g_last[gr]
                     - _dot(s_old[gr].astype(BF16), jnp.where(bd_gg, m_full[gr], 0.0).astype(BF16)) + n_h)
        s_scr[gr[0], gr[1]] = s_new[gr]

    @pl.when(c == pl.num_programs(1) - 1)
    def _():
        for j, g in groups:
            for i in range(4):
                s_out_ref[j, 4 * g + i] = s_new[(j, g)][:, i * N_B:(i + 1) * N_B]

    def head_means(xs):
        pieces = []
        for gr in groups:
            hi = xs[gr].astype(BF16).astype(F32)
            pieces += [hi, xs[gr] - hi]
        tot = _dot(jnp.concatenate(pieces, axis=0).astype(BF16), ones_gg) * (1.0 / N_B)
        return {gr: tot[2 * i * chunk:(2 * i + 1) * chunk] + tot[(2 * i + 1) * chunk:(2 * i + 2) * chunk]
                for i, gr in enumerate(groups)}

    mean = head_means(y)
    dev = {gr: y[gr] - mean[gr] for gr in groups}
    var = head_means({gr: dev[gr] * dev[gr] for gr in groups})
    for j in range(n_sb):
        outs = [dev[(j, g)] * lax.rsqrt(var[(j, g)] + GN_EPS) for g in range(n_g)]
        yn = jnp.concatenate(outs, axis=-1) * lnw_ref[...] + lnb_ref[...]
        o_ref[j] = (yn + bonus_ref[j]) * g_ref[j]


def _rwkv_scan(streams, ln_w, ln_b, s0, chunk, n_sb):
    _, n_seq, t, _ = streams.shape
    n_chunks = t // chunk
    tri = jnp.tril(jnp.ones((chunk, chunk), F32)).astype(BF16)
    row = pl.BlockSpec((n_sb, chunk, D_B), lambda b, c: (b, c, 0))
    st = pl.BlockSpec((n_sb, H_B, N_B, N_B), lambda b, c: (b, 0, 0, 0))
    stream = lambda k: pl.BlockSpec((None, n_sb, chunk, D_B), lambda b, c: (k, b, c, 0))
    return pl.pallas_call(
        functools.partial(_rwkv_scan_kernel, chunk=chunk, n_sb=n_sb),
        grid=(n_seq // n_sb, n_chunks),
        in_specs=[stream(k) for k in range(N_STREAMS)] + [_full(ln_w.shape), _full(ln_b.shape), st, _full(tri.shape)],
        out_specs=[row, st],
        out_shape=[jax.ShapeDtypeStruct((n_seq, t, D_B), F32), jax.ShapeDtypeStruct(s0.shape, F32)],
        scratch_shapes=[pltpu.VMEM((n_sb, H_B // 4, N_B, 4 * N_B), F32)],
        compiler_params=_cparams(("parallel", "arbitrary")),
        name="rwkv_scan",
    )(*([streams] * N_STREAMS), ln_w, ln_b, s0, tri)


def _pool_prompt_kernel(x_ref, gpre_ref, wpool_ref, pscale_ref, gpost_ref, o_ref, tail_ref, hext, *, tm):
    j = pl.program_id(1)
    x = x_ref[...]
    h = _rms(x, gpre_ref[...])

    @pl.when(j == 0)
    def _():
        hext[0:W_MAX, :] = jnp.zeros((W_MAX, D_MODEL), F32)

    @pl.when(j > 0)
    def _():
        hext[0:W_MAX, :] = hext[tm:tm + W_MAX, :]

    hext[W_MAX:, :] = h
    tail_ref[0] = h[tm - W_MAX:, :]
    pos = j * tm + lax.broadcasted_iota(jnp.int32, (tm, 1), 0)
    acc = hext[...]
    ys = []
    for gi, w in enumerate(POOL_WINDOWS):
        acc = acc[:, POOL_GC * (1 if gi else 0):]
        acc = acc + pltpu.roll(acc, w // 2, 0)
        cnt = jnp.minimum(pos + 1, w).astype(F32)
        pooled = acc[W_MAX:, :POOL_GC] / cnt - h[:, gi * POOL_GC:(gi + 1) * POOL_GC]
        ys.append(_dot(pooled.astype(BF16), wpool_ref[gi]))
    y = jnp.concatenate(ys, axis=-1) * pscale_ref[...]
    o_ref[...] = x + _rms(y, gpost_ref[...])


def _pool_prompt(x, g_pre, w_pool, p_scale, g_post, n_b, t_p):
    m = x.shape[0]
    tm = POOL_TILE
    n_t = t_p // tm
    row = pl.BlockSpec((tm, D_MODEL), lambda b, j: (b * n_t + j, 0))
    return pl.pallas_call(
        functools.partial(_pool_prompt_kernel, tm=tm),
        grid=(n_b, n_t),
        in_specs=[row, _full(g_pre.shape), _full(w_pool.shape), _full(p_scale.shape), _full(g_post.shape)],
        out_specs=[row, pl.BlockSpec((1, W_MAX, D_MODEL), lambda b, j: (b, 0, 0))],
        out_shape=[jax.ShapeDtypeStruct((m, D_MODEL), F32), jax.ShapeDtypeStruct((n_b, W_MAX, D_MODEL), F32)],
        scratch_shapes=[pltpu.VMEM((W_MAX + tm, D_MODEL), F32)],
        compiler_params=_cparams(("parallel", "arbitrary")),
        name="pool_prompt",
    )(x, g_pre, w_pool, p_scale, g_post)


def _pool_sample_kernel(x_ref, pre_ref, gpre_ref, wpool_ref, pscale_ref, gpost_ref, o_ref, h_ref, *, n_b, t_s):
    x = x_ref[...]
    h = _rms(x, gpre_ref[...])
    h_ref[...] = h
    n_pre = W_MAX - 1
    rows = [pre_ref[i * n_b:(i + 1) * n_b, :] for i in range(n_pre)] + [h[t * n_b:(t + 1) * n_b, :] for t in range(t_s)]
    outs = []
    for t in range(t_s):
        ys = []
        for gi, w in enumerate(POOL_WINDOWS):
            sl = slice(gi * POOL_GC, (gi + 1) * POOL_GC)
            win = rows[n_pre + t][:, sl]
            for d in range(1, w):
                win = win + rows[n_pre + t - d][:, sl]
            pooled = win / float(w) - rows[n_pre + t][:, sl]
            ys.append(_dot(pooled.astype(BF16), wpool_ref[gi]))
        outs.append(jnp.concatenate(ys, axis=-1))
    y = jnp.concatenate(outs, axis=0) * pscale_ref[...]
    o_ref[...] = x + _rms(y, gpost_ref[...])


def _pool_sample(x, prefix, g_pre, w_pool, p_scale, g_post, n_b, t_s):
    args = (x, prefix, g_pre, w_pool, p_scale, g_post)
    return pl.pallas_call(
        functools.partial(_pool_sample_kernel, n_b=n_b, t_s=t_s),
        grid=(1,),
        in_specs=[_full(a.shape) for a in args],
        out_specs=[_full(x.shape), _full(x.shape)],
        out_shape=[jax.ShapeDtypeStruct(x.shape, F32)] * 2,
        compiler_params=_cparams(("arbitrary",)),
        name="pool_sample",
    )(*args)


def _rope_tables(pos):
    half = ROPE_DIM // 2
    inv = ROPE_BASE ** (-jnp.arange(half, dtype=F32) / half)
    ang = pos[:, None] * inv[None, :]
    cos, sin = jnp.cos(ang), jnp.sin(ang)
    reps = LANES // ROPE_DIM
    return jnp.tile(jnp.concatenate([cos, cos], -1), (1, reps)), jnp.tile(jnp.concatenate([-sin, sin], -1), (1, reps))


def _swap_halves(w):
    half = ROPE_DIM // 2
    return jnp.concatenate([w[..., half:], w[..., :half]], axis=-1)


def _ab_weights(e, w_in, g_q, w_uq, g_kv, w_uk):
    w = w_in[e]
    w_pe = w[:, Q_LORA + KV_LORA:MLA_IN]
    wq2 = jnp.concatenate([w[:, :Q_LORA], w_pe, _swap_halves(w_pe)], axis=1).astype(BF16)
    wkv = w[:, Q_LORA:Q_LORA + KV_LORA].astype(BF16)
    wb = w[:, MLA_IN:].astype(BF16)
    uq = w_uq[e].reshape(Q_LORA, H_A, NOPE_DIM + ROPE_DIM)
    uq_pe = uq[:, :, NOPE_DIM:]
    wuq = jnp.concatenate([uq[:, :, :NOPE_DIM].reshape(Q_LORA, H_A * NOPE_DIM),
                           uq_pe.reshape(Q_LORA, H_A * ROPE_DIM),
                           _swap_halves(uq_pe).reshape(Q_LORA, H_A * ROPE_DIM)], axis=1).astype(BF16)
    wuk = jnp.transpose(w_uk[e], (1, 2, 0)).astype(BF16)
    src = jnp.arange(H_A * ROPE_DIM)
    dst = (src // ROPE_DIM) * LANES + src % ROPE_DIM
    spread = (dst[:, None] == jnp.arange(H_A * LANES)[None, :]).astype(BF16)
    return wq2, wkv, wb, g_q[e][None], g_kv[e][None], wuq, wuk, spread


def _rwkv_weights(e, mu_shift, w0, w2, a0, a2, g2, k_k, k_a, r_k):
    head = jnp.arange(D_B) // N_B
    ones = (head[:, None] == head[None, :]).astype(BF16)
    return (mu_shift[e][None], w0[e][None], w2[e], a0[e][None], a2[e], g2[e],
            k_k[e][None], k_a[e][None], r_k[e].reshape(1, D_B), ones)


def kernel(x_prompt, x_sample, cache_ckv, cache_kpe, page_table, state_wkv, state_shift, state_pool, meta_tokens,
           g_mix_pre, g_mix_post, g_ffn_pre, g_ffn_post, w_in, g_q, w_uq, g_kv, w_uk, w_uv, mu_shift, w0, w2, a0, a2,
           g2, k_k, k_a, r_k, ln_w, ln_b, w_out, w_pool, pool_scale, w_up, w_down):
    n_bp, seq, _ = x_prompt.shape
    n_bs, t_s, _ = x_sample.shape
    depth = g_mix_pre.shape[0]
    t_p = seq + N_META
    n_pages = page_table.shape[1]
    past = n_pages * PAGE_SIZE
    m_p = n_bp * t_p
    m_s = n_bs * t_s

    meta = jnp.broadcast_to(meta_tokens[None].astype(x_prompt.dtype), (n_bp, N_META, D_MODEL))
    xp = jnp.concatenate([meta, x_prompt], axis=1).reshape(m_p, D_MODEL)
    xs = jnp.transpose(x_sample, (1, 0, 2)).reshape(m_s, D_MODEL)

    tm_s = m_s
    cos_p, sin_p = _rope_tables((jnp.arange(t_p + ROW_TILE) % t_p).astype(F32))
    cos_s, sin_s = _rope_tables((past + (jnp.arange(m_s + tm_s) % m_s) // n_bs).astype(F32))
    cache_kpe_t = jnp.swapaxes(cache_kpe, 2, 3)

    ckv_p, kpe_p, wkv_p, shift_p, pool_p = [], [], [], [], []
    ckv_s, kpe_s, wkv_s, shift_s, pool_s = [], [], [], [], []
    for l in range(depth):
        g_pre, g_post = g_mix_pre[l][None], g_mix_post[l][None]
        if l % 2 == 0:
            e = l // 2
            ab_w = _ab_weights(e, w_in, g_q, w_uq, g_kv, w_uk)
            rw = _rwkv_weights(e, mu_shift, w0, w2, a0, a2, g2, k_k, k_a, r_k)
            wuv = jnp.transpose(w_uv[e], (1, 0, 2)).astype(BF16)
            wout = w_out[e].astype(BF16)
            lnw, lnb = ln_w[e][None], ln_b[e][None]

            ckv, kpe, zb, qabs, kvb = _ab_in_proj(xp, g_pre, ab_w, cos_p, sin_p, ROW_TILE)
            olat = _attn_prompt(qabs, kvb, n_bp, t_p)
            zb3 = zb.reshape(n_bp, t_p, RWKV_IN)
            first = jnp.zeros((n_bp, 8, RWKV_IN), F32)
            streams = _rwkv_prep(zb, first, rw, PREP_TILE, 1).reshape(N_STREAMS, n_bp, t_p, D_B)
            o_b, s_fin = _rwkv_scan(streams, lnw, lnb, jnp.zeros((n_bp, H_B, N_B, N_B), F32), SCAN_CHUNK_P,
                                    SCAN_SEQS_P)
            xp = _ab_out_proj(olat, o_b.reshape(m_p, D_B), xp, wuv, wout, g_post, ROW_TILE)
            ckv_p.append(ckv.reshape(n_bp, t_p, KV_LORA))
            kpe_p.append(kpe.reshape(n_bp, t_p, ROPE_DIM))
            wkv_p.append(s_fin)
            shift_p.append(zb3[:, -1])

            ckv, kpe, zb, qabs, kvb = _ab_in_proj(xs, g_pre, ab_w, cos_s, sin_s, tm_s)
            q_s = jnp.transpose(qabs.reshape(H_A, t_s, n_bs, QK_PAD), (2, 0, 1, 3)).reshape(n_bs, H_A * t_s, QK_PAD)
            k_new = jnp.pad(jnp.transpose(kvb.reshape(t_s, n_bs, QK_PAD), (1, 0, 2)),
                            ((0, 0), (0, NEW_KEY_ROWS - t_s), (0, 0)))
            o_s = _attn_sample(page_table, q_s, k_new, cache_ckv, cache_kpe_t, e, t_s)
            olat = jnp.transpose(o_s.reshape(n_bs, H_A, t_s, KV_LORA), (1, 2, 0, 3)).reshape(H_A, m_s, KV_LORA)
            streams = _rwkv_prep(zb, state_shift[e].astype(F32)[None], rw, tm_s, n_bs)
            streams = jnp.pad(jnp.transpose(streams.reshape(N_STREAMS, t_s, n_bs, D_B), (0, 2, 1, 3)),
                              ((0, 0), (0, 0), (0, SCAN_CHUNK_S - t_s), (0, 0)))
            o_b, s_fin = _rwkv_scan(streams, lnw, lnb, state_wkv[e].astype(F32), SCAN_CHUNK_S, SCAN_SEQS_S)
            o_b = jnp.transpose(o_b[:, :t_s], (1, 0, 2)).reshape(m_s, D_B)
            xs = _ab_out_proj(olat, o_b, xs, wuv, wout, g_post, tm_s)
            ckv_s.append(jnp.transpose(ckv.reshape(t_s, n_bs, KV_LORA), (1, 0, 2)))
            kpe_s.append(jnp.transpose(kpe.reshape(t_s, n_bs, ROPE_DIM), (1, 0, 2)))
            wkv_s.append(s_fin)
            shift_s.append(zb[m_s - n_bs:])
        else:
            o = l // 2
            wp = w_pool[o].astype(BF16)
            ps = pool_scale[o][None]
            xp, tail = _pool_prompt(xp, g_pre, wp, ps, g_post, n_bp, t_p)
            pool_p.append(tail[:, 1:])
            prefix = jnp.transpose(state_pool[o].astype(F32), (1, 0, 2))
            xs, h_s = _pool_sample(xs, prefix.reshape((W_MAX - 1) * n_bs, D_MODEL), g_pre, wp, ps, g_post, n_bs, t_s)
            full = jnp.concatenate([prefix, h_s.reshape(t_s, n_bs, D_MODEL)], axis=0)
            pool_s.append(jnp.transpose(full[-(W_MAX - 1):], (1, 0, 2)))
        gfp, gfo = g_ffn_pre[l][None], g_ffn_post[l][None]
        xp, xs = _ffn(xp, xs, gfp, w_up, w_down, gfo, l, ROW_TILE)

    y_prompt = xp.reshape(n_bp, t_p, D_MODEL)[:, N_META:]
    y_sample = jnp.transpose(xs.reshape(t_s, n_bs, D_MODEL), (1, 0, 2))
    return (y_prompt, y_sample, jnp.stack(ckv_p), jnp.stack(kpe_p), jnp.stack(wkv_p), jnp.stack(shift_p),
            jnp.stack(pool_p), jnp.stack(ckv_s), jnp.stack(kpe_s), jnp.stack(wkv_s), jnp.stack(shift_s),
            jnp.stack(pool_s))
```
